```python
import math
import jax, jax.numpy as jnp
from jax import lax
import numpy as np

D_MODEL = 1024
BATCH = 16
SEQ = 4096
DEPTH = 1

MLA_HEADS = 8
MLA_NOPE = 64
MLA_ROPE = 32
MLA_V = 64
Q_LORA = 256
KV_LORA = 128
ROPE_THETA = 10000.0
ATTN_Q_BLOCK = 128
MOBA_HEADS = 8
MOBA_HD = 64
MOBA_BLOCK = 256
MOBA_TOPK = 3
MOBA_Q_CHUNK = 16
D_FF = 2816
CONV_W = 3
MIX_W = MLA_HEADS * MLA_V + MOBA_HEADS * MOBA_HD
MOBA_W = MOBA_HEADS * MOBA_HD
IN_COLS = Q_LORA + KV_LORA + MLA_ROPE + 3 * MOBA_W
SPLITS = [Q_LORA, Q_LORA + KV_LORA, Q_LORA + KV_LORA + MLA_ROPE,
          Q_LORA + KV_LORA + MLA_ROPE + MOBA_W, Q_LORA + KV_LORA + MLA_ROPE + 2 * MOBA_W]
EPS = 1e-5

kernel_name = "hymba_mla_moba_deepnorm_convffn"


def layer_norm(x, g, b):
    xf = x.astype(jnp.float32)
    mu = jnp.mean(xf, -1, keepdims=True)
    var = jnp.mean(jnp.square(xf - mu), -1, keepdims=True)
    return ((xf - mu) * lax.rsqrt(var + EPS) * g.astype(jnp.float32) + b.astype(jnp.float32)).astype(x.dtype)


def rms_norm(x, g):
    xf = x.astype(jnp.float32)
    ms = jnp.mean(jnp.square(xf), -1, keepdims=True)
    return (xf * lax.rsqrt(ms + EPS) * g.astype(jnp.float32)).astype(x.dtype)


def rope(x):
    S, R = x.shape[1], x.shape[-1]
    half = R // 2
    inv = ROPE_THETA ** (-jnp.arange(half, dtype=jnp.float32) / half)
    ang = jnp.arange(S, dtype=jnp.float32)[:, None] * inv[None, :]
    cos = jnp.cos(ang)[None, :, None, :].astype(x.dtype)
    sin = jnp.sin(ang)[None, :, None, :].astype(x.dtype)
    x1, x2 = x[..., :half], x[..., half:]
    return jnp.concatenate([x1 * cos - x2 * sin, x1 * sin + x2 * cos], -1)


def mla_attention(c_q, c_kv, k_rope, q_norm_g, w_uq, kv_norm_g, w_ukv):
    B, S, _ = c_q.shape
    q = (rms_norm(c_q, q_norm_g) @ w_uq).reshape(B, S, MLA_HEADS, MLA_NOPE + MLA_ROPE)
    q_nope, q_pe = q[..., :MLA_NOPE], rope(q[..., MLA_NOPE:])
    kv = (rms_norm(c_kv, kv_norm_g) @ w_ukv).reshape(B, S, MLA_HEADS, MLA_NOPE + MLA_V)
    k_nope, v = kv[..., :MLA_NOPE], kv[..., MLA_NOPE:]
    k_pe = rope(k_rope[:, :, None, :])[:, :, 0]
    scale = (MLA_NOPE + MLA_ROPE) ** -0.5
    nq = S // ATTN_Q_BLOCK
    blk = lambda a: a.reshape(B, nq, ATTN_Q_BLOCK, *a.shape[2:]).swapaxes(0, 1)
    kpos = jnp.arange(S)

    def one(args):
        i, qn, qp = args
        s = (jnp.einsum('bthd,bshd->bhts', qn, k_nope)
             + jnp.einsum('bthr,bsr->bhts', qp, k_pe)).astype(jnp.float32) * scale
        qpos = i * ATTN_Q_BLOCK + jnp.arange(ATTN_Q_BLOCK)
        s = jnp.where(kpos[None, :] <= qpos[:, None], s, -jnp.inf)
        p = jax.nn.softmax(s, -1).astype(v.dtype)
        return jnp.einsum('bhts,bshd->bthd', p, v)

    out = lax.map(one, (jnp.arange(nq), blk(q_nope), blk(q_pe)))
    return out.swapaxes(0, 1).reshape(B, S, MLA_HEADS * MLA_V)


def moba_attention(q, k, v):
    B, S, H, D = q.shape
    NB = -(-S // MOBA_BLOCK)
    Sp = NB * MOBA_BLOCK
    pad = ((0, 0), (0, Sp - S), (0, 0), (0, 0))
    q, k, v = jnp.pad(q, pad), jnp.pad(k, pad), jnp.pad(v, pad)
    kb = k.reshape(B, NB, MOBA_BLOCK, H, D).transpose(0, 3, 1, 2, 4)
    vb = v.reshape(B, NB, MOBA_BLOCK, H, D).transpose(0, 3, 1, 2, 4)
    kmean = jnp.mean(kb.astype(jnp.float32), axis=3)
    topk = min(MOBA_TOPK, NB)
    L = MOBA_BLOCK
    Tq = MOBA_Q_CHUNK
    slopes = 2.0 ** (-8.0 * jnp.arange(1, H + 1, dtype=jnp.float32) / H)
    scale = D ** -0.5
    nc = Sp // Tq
    qc = q.reshape(B, nc, Tq, H, D).transpose(1, 0, 3, 2, 4)
    bi = jnp.arange(B)[:, None, None, None]
    hi = jnp.arange(H)[None, :, None, None]

    def one(args):
        i, qi = args
        t = i * Tq + jnp.arange(Tq)
        own = (i * Tq) // MOBA_BLOCK
        g = jnp.einsum('bhtd,bhnd->bhtn', qi.astype(jnp.float32), kmean)
        g = jnp.where(jnp.arange(NB) < own, g, -jnp.inf)
        _, sel = lax.top_k(g, topk)
        valid = jnp.arange(topk) < own
        k_sel = kb[bi, hi, sel]
        v_sel = vb[bi, hi, sel]
        s_sel = jnp.einsum('bhtd,bhtjld->bhtjl', qi, k_sel).astype(jnp.float32) * scale
        kpos_sel = sel[..., None] * MOBA_BLOCK + jnp.arange(L)
        dist_sel = (t[:, None, None] - kpos_sel).astype(jnp.float32)
        s_sel = jnp.where(valid[:, None], s_sel - slopes[:, None, None, None] * dist_sel, -jnp.inf)
        k_own = lax.dynamic_index_in_dim(kb, own, axis=2, keepdims=False)
        v_own = lax.dynamic_index_in_dim(vb, own, axis=2, keepdims=False)
        s_own = jnp.einsum('bhtd,bhld->bhtl', qi, k_own).astype(jnp.float32) * scale
        dist_own = (t[:, None] - (own * MOBA_BLOCK + jnp.arange(L))[None, :]).astype(jnp.float32)
        s_own = jnp.where(dist_own >= 0, s_own - slopes[:, None, None] * dist_own, -jnp.inf)
        s = jnp.concatenate([s_sel.reshape(B, H, Tq, topk * L), s_own], -1)
        p = jax.nn.softmax(s, -1).astype(v.dtype)
        p_sel = p[..., :topk * L].reshape(B, H, Tq, topk, L)
        return (jnp.einsum('bhtjl,bhtjld->bhtd', p_sel, v_sel)
                + jnp.einsum('bhtl,bhld->bhtd', p[..., topk * L:], v_own))

    out = lax.map(one, (jnp.arange(nc), qc))
    return out.transpose(1, 0, 3, 2, 4).reshape(B, Sp, H * D)[:, :S]


def conv_gated_ffn(x, w_up, conv_w, conv_b, w_down):
    h = x @ w_up
    C = h.shape[-1]
    h = lax.conv_general_dilated(h, conv_w[:, None, :].astype(h.dtype), window_strides=(1,),
                                 padding=[(CONV_W - 1, 0)],
                                 dimension_numbers=('NWC', 'WIO', 'NWC'),
                                 feature_group_count=C) + conv_b
    gate, up = jnp.split(h, 2, axis=-1)
    return (jax.nn.silu(gate) * up) @ w_down


def setup_inputs(seed: int = 0) -> dict:
    key = jax.random.key(seed)
    ks = jax.random.split(key, 24)
    beta = (8.0 * DEPTH) ** -0.25
    nrm = lambda k, shape, s: jax.random.normal(k, shape, jnp.float32) * s
    x = jax.random.normal(ks[0], (BATCH, SEQ, D_MODEL), jnp.float32)
    w_in = jnp.concatenate([
        nrm(ks[1], (DEPTH, D_MODEL, Q_LORA + KV_LORA + MLA_ROPE + 2 * MOBA_W), D_MODEL ** -0.5),
        nrm(ks[2], (DEPTH, D_MODEL, MOBA_W), beta * D_MODEL ** -0.5)], axis=-1)
    w_ukv = jnp.concatenate([
        nrm(ks[3], (DEPTH, KV_LORA, MLA_HEADS, MLA_NOPE), KV_LORA ** -0.5),
        nrm(ks[4], (DEPTH, KV_LORA, MLA_HEADS, MLA_V), beta * KV_LORA ** -0.5)],
        axis=-1).reshape(DEPTH, KV_LORA, MLA_HEADS * (MLA_NOPE + MLA_V))
    return {
        "x": x,
        "w_in": w_in,
        "q_norm_g": 1.0 + nrm(ks[5], (DEPTH, Q_LORA), 0.01),
        "w_uq": nrm(ks[6], (DEPTH, Q_LORA, MLA_HEADS * (MLA_NOPE + MLA_ROPE)), Q_LORA ** -0.5),
        "kv_norm_g": 1.0 + nrm(ks[7], (DEPTH, KV_LORA), 0.01),
        "w_ukv": w_ukv,
        "w_o": nrm(ks[8], (DEPTH, MIX_W, D_MODEL), beta * MIX_W ** -0.5),
        "ln1_g": 1.0 + nrm(ks[9], (DEPTH, D_MODEL), 0.01),
        "ln1_b": nrm(ks[10], (DEPTH, D_MODEL), 0.01),
        "w_up": nrm(ks[11], (DEPTH, D_MODEL, 2 * D_FF), D_MODEL ** -0.5),
        "conv_w": nrm(ks[12], (DEPTH, CONV_W, 2 * D_FF), CONV_W ** -0.5),
        "conv_b": nrm(ks[13], (DEPTH, 2 * D_FF), 0.01),
        "w_down": nrm(ks[14], (DEPTH, D_FF, D_MODEL), beta * D_FF ** -0.5),
        "ln2_g": 1.0 + nrm(ks[15], (DEPTH, D_MODEL), 0.01),
        "ln2_b": nrm(ks[16], (DEPTH, D_MODEL), 0.01),
    }


def reference(x, w_in, q_norm_g, w_uq, kv_norm_g, w_ukv, w_o, ln1_g, ln1_b,
              w_up, conv_w, conv_b, w_down, ln2_g, ln2_b):
    alpha = (2.0 * DEPTH) ** 0.25
    B, S, _ = x.shape
    for l in range(DEPTH):
        proj = x @ w_in[l]
        c_q, c_kv, k_rope, q_m, k_m, v_m = jnp.split(proj, SPLITS, axis=-1)
        a_out = mla_attention(c_q, c_kv, k_rope, q_norm_g[l], w_uq[l], kv_norm_g[l], w_ukv[l])
        b_out = moba_attention(q_m.reshape(B, S, MOBA_HEADS, MOBA_HD),
                               k_m.reshape(B, S, MOBA_HEADS, MOBA_HD),
                               v_m.reshape(B, S, MOBA_HEADS, MOBA_HD))
        mix = jnp.concatenate([a_out, b_out], axis=-1) @ w_o[l]
        x = layer_norm(alpha * x + mix, ln1_g[l], ln1_b[l])
        x = layer_norm(alpha * x + conv_gated_ffn(x, w_up[l], conv_w[l], conv_b[l], w_down[l]),
                       ln2_g[l], ln2_b[l])
    return x
```

```python
import functools

import jax
import jax.numpy as jnp
from jax import lax
from jax.experimental import pallas as pl
from jax.experimental.pallas import tpu as pltpu

MLA_HEADS = 8
MLA_NOPE = 64
MLA_ROPE = 32
MLA_V = 64
ROPE_THETA = 10000.0
MOBA_HEADS = 8
MOBA_HD = 64
MOBA_BLOCK = 256
MOBA_TOPK = 3
CONV_W = 3
EPS = 1e-5

LANES = 128
HEAD_PAD = 128
HEADS_PER_STEP = 2
ATTN_TILE = MOBA_BLOCK
MASKED = -1e30
PROJ_ROWS = 512
FFN_ROWS = 512
FFN_CHUNK = 256
CARRY_ROWS = 8
VMEM_LIMIT = 48 * 1024 * 1024

F32 = jnp.float32
BF16 = jnp.bfloat16

_dot = functools.partial(jnp.dot, preferred_element_type=F32)


def _rms_norm(x, g):
    ms = jnp.mean(x * x, axis=-1, keepdims=True)
    return x * lax.rsqrt(ms + EPS) * g


def _layer_norm(x, g, b):
    mu = jnp.mean(x, axis=-1, keepdims=True)
    xc = x - mu
    var = jnp.mean(xc * xc, axis=-1, keepdims=True)
    return xc * lax.rsqrt(var + EPS) * g + b


def _proj_kernel(x_ref, win_ref, qg_ref, wqa_ref, wqb_ref, kvg_ref, wk_ref, wv_ref,
                 tq1_ref, tq2_ref, tk1_ref, tk2_ref,
                 qat_ref, ka_ref, vat_ref, qmt_ref, km_ref, vmt_ref, kmean_ref):
    q_lora = wqa_ref.shape[0]
    kv_lora = wk_ref.shape[0]
    moba_w = km_ref.shape[2]
    rows = x_ref.shape[1]
    c0 = q_lora + kv_lora
    c1 = c0 + HEAD_PAD

    xb = x_ref[0].astype(BF16)
    c_q = _dot(xb, win_ref[:, 0:q_lora])
    c_kv = _dot(xb, win_ref[:, q_lora:c0])
    rope_grp = _dot(xb, win_ref[:, c0:c1])
    qm = _dot(xb, win_ref[:, c1:c1 + moba_w]) * (MOBA_HD ** -0.5)
    km = _dot(xb, win_ref[:, c1 + moba_w:c1 + 2 * moba_w])
    vm = _dot(xb, win_ref[:, c1 + 2 * moba_w:c1 + 3 * moba_w])

    cqn = _rms_norm(c_q, qg_ref[...]).astype(BF16)
    ckn = _rms_norm(c_kv, kvg_ref[...]).astype(BF16)
    qa = _dot(cqn, wqa_ref[...])
    qb = _dot(cqn, wqb_ref[...])
    kk = _dot(ckn, wk_ref[...])
    vv = _dot(ckn, wv_ref[...])

    k_pe = rope_grp * tk1_ref[...] + pltpu.roll(rope_grp, HEAD_PAD - MLA_ROPE, 1) * tk2_ref[...]
    tq1 = tq1_ref[...]
    tq2 = tq2_ref[...]
    for h in range(MLA_HEADS):
        sl = slice(h * HEAD_PAD, (h + 1) * HEAD_PAD)
        q_h = qa[:, sl] * tq1 + qb[:, sl] * tq2
        qat_ref[0, sl, :] = q_h.T.astype(BF16)
        ka_ref[0, :, sl] = (kk[:, sl] + k_pe).astype(BF16)

    qmt_ref[0] = qm.T.astype(BF16)
    km_ref[0] = km.astype(BF16)
    for j in range(rows // MOBA_BLOCK):
        blk = slice(j * MOBA_BLOCK, (j + 1) * MOBA_BLOCK)
        vat_ref[0, j] = vv[blk, :].T.astype(BF16)
        vmt_ref[0, j] = vm[blk, :].T.astype(BF16)
        kmean_ref[0, j] = jnp.mean(km[blk, :], axis=0, keepdims=True)


def _rope_tables(seq):
    half = MLA_ROPE // 2
    inv = ROPE_THETA ** (-jnp.arange(half, dtype=F32) / half)
    ang = jnp.arange(seq, dtype=F32)[:, None] * inv[None, :]
    cos2 = jnp.concatenate([jnp.cos(ang), jnp.cos(ang)], axis=1)
    sin2 = jnp.concatenate([jnp.sin(ang), jnp.sin(ang)], axis=1)
    lo = jnp.zeros((seq, MLA_NOPE), F32)
    hi = jnp.zeros((seq, HEAD_PAD - MLA_NOPE - MLA_ROPE), F32)
    tk1 = jnp.concatenate([lo, cos2, hi], axis=1)
    tk2 = jnp.concatenate([lo, sin2, hi], axis=1)
    scale = (MLA_NOPE + MLA_ROPE) ** -0.5
    tq1 = jnp.concatenate([lo + 1.0, cos2, hi], axis=1) * scale
    tq2 = tk2 * scale
    return tq1, tq2, tk1, tk2


def _rotate_half_cols(w):
    half = w.shape[-1] // 2
    return jnp.concatenate([-w[..., half:], w[..., :half]], axis=-1)


def _prep_proj_weights(w_in, w_uq, w_ukv):
    d_model = w_in.shape[0]
    q_lora = w_uq.shape[0]
    kv_lora = w_ukv.shape[0]
    moba_w = MOBA_HEADS * MOBA_HD
    o = q_lora + kv_lora
    k_rope = w_in[:, o:o + MLA_ROPE]
    rope_grp = jnp.concatenate(
        [jnp.zeros((d_model, MLA_NOPE), w_in.dtype), k_rope, _rotate_half_cols(k_rope)], axis=1)
    o += MLA_ROPE
    win = jnp.concatenate([w_in[:, :q_lora + kv_lora], rope_grp, w_in[:, o:o + 3 * moba_w]], axis=1)

    wq = w_uq.reshape(q_lora, MLA_HEADS, MLA_NOPE + MLA_ROPE)
    pad = jnp.zeros((q_lora, MLA_HEADS, HEAD_PAD - MLA_NOPE - MLA_ROPE), w_uq.dtype)
    wqa = jnp.concatenate([wq, pad], axis=-1).reshape(q_lora, MLA_HEADS * HEAD_PAD)
    wqb = jnp.concatenate([jnp.zeros((q_lora, MLA_HEADS, MLA_NOPE), w_uq.dtype),
                           _rotate_half_cols(wq[..., MLA_NOPE:]), pad], axis=-1)
    wqb = wqb.reshape(q_lora, MLA_HEADS * HEAD_PAD)

    wkv = w_ukv.reshape(kv_lora, MLA_HEADS, MLA_NOPE + MLA_V)
    wk = jnp.concatenate([wkv[..., :MLA_NOPE],
                          jnp.zeros((kv_lora, MLA_HEADS, HEAD_PAD - MLA_NOPE), w_ukv.dtype)], axis=-1)
    wk = wk.reshape(kv_lora, MLA_HEADS * HEAD_PAD)
    wv = wkv[..., MLA_NOPE:].reshape(kv_lora, MLA_HEADS * MLA_V)
    return win.astype(BF16), wqa.astype(BF16), wqb.astype(BF16), wk.astype(BF16), wv.astype(BF16)


def _const_spec(shape):
    return pl.BlockSpec(shape, lambda *_: (0,) * len(shape))


def _input_projection(x, w_in, q_norm_g, w_uq, kv_norm_g, w_ukv):
    B, S, D = x.shape
    rows = PROJ_ROWS
    assert S % rows == 0 and rows % MOBA_BLOCK == 0
    nb = S // MOBA_BLOCK
    win, wqa, wqb, wk, wv = _prep_proj_weights(w_in, w_uq, w_ukv)
    tables = _rope_tables(S)
    mla_w = MLA_HEADS * HEAD_PAD
    v_w = MLA_HEADS * MLA_V
    moba_w = MOBA_HEADS * MOBA_HD
    blocks_per_step = rows // MOBA_BLOCK

    table_spec = pl.BlockSpec((rows, HEAD_PAD), lambda b, i: (i, 0))
    in_specs = [
        pl.BlockSpec((1, rows, D), lambda b, i: (b, i, 0)),
        _const_spec(win.shape),
        _const_spec((1, q_norm_g.shape[-1])), _const_spec(wqa.shape), _const_spec(wqb.shape),
        _const_spec((1, kv_norm_g.shape[-1])), _const_spec(wk.shape), _const_spec(wv.shape),
        table_spec, table_spec, table_spec, table_spec,
    ]
    out_shape = [
        jax.ShapeDtypeStruct((B, mla_w, S), BF16),
        jax.ShapeDtypeStruct((B, S, mla_w), BF16),
        jax.ShapeDtypeStruct((B, nb, v_w, MOBA_BLOCK), BF16),
        jax.ShapeDtypeStruct((B, moba_w, S), BF16),
        jax.ShapeDtypeStruct((B, S, moba_w), BF16),
        jax.ShapeDtypeStruct((B, nb, moba_w, MOBA_BLOCK), BF16),
        jax.ShapeDtypeStruct((B, nb, 1, moba_w), F32),
    ]
    out_specs = [
        pl.BlockSpec((1, mla_w, rows), lambda b, i: (b, 0, i)),
        pl.BlockSpec((1, rows, mla_w), lambda b, i: (b, i, 0)),
        pl.BlockSpec((1, blocks_per_step, v_w, MOBA_BLOCK), lambda b, i: (b, i, 0, 0)),
        pl.BlockSpec((1, moba_w, rows), lambda b, i: (b, 0, i)),
        pl.BlockSpec((1, rows, moba_w), lambda b, i: (b, i, 0)),
        pl.BlockSpec((1, blocks_per_step, moba_w, MOBA_BLOCK), lambda b, i: (b, i, 0, 0)),
        pl.BlockSpec((1, blocks_per_step, 1, moba_w), lambda b, i: (b, i, 0, 0)),
    ]
    return pl.pallas_call(
        _proj_kernel,
        grid=(B, S // rows),
        in_specs=in_specs,
        out_specs=out_specs,
        out_shape=out_shape,
        compiler_params=pltpu.CompilerParams(
            dimension_semantics=("parallel", "parallel"), vmem_limit_bytes=VMEM_LIMIT),
        name="input_projection",
    )(x, win, q_norm_g.reshape(1, -1), wqa, wqb, kv_norm_g.reshape(1, -1), wk, wv, *tables)


def _moba_gate_bias(kmean, qt, own, slope):
    nb = kmean.shape[0]
    g = _dot(kmean, qt)
    blk = lax.broadcasted_iota(jnp.int32, g.shape, 0)
    past = blk < own
    g = jnp.where(past, g, -jnp.inf)
    rank = jnp.zeros(g.shape, F32)
    for m in range(nb):
        row = g[m:m + 1, :]
        beats = (row > g) | ((row == g) & (blk > m))
        rank = rank + jnp.where(beats, 1.0, 0.0)
    selected = past & (rank < MOBA_TOPK)
    block_dist = (own - blk).astype(F32) * MOBA_BLOCK
    return jnp.where(selected, 0.0, MASKED) - slope * block_dist


def _attn_kernel(*refs, moba):
    if moba:
        slopes_ref, qt_ref, k_ref, vt_ref, kmean_ref, o_ref, bias_ref = refs
    else:
        qt_ref, k_ref, vt_ref, o_ref = refs
    tile = ATTN_TILE
    pair = pl.program_id(1)
    own = pl.program_id(2)
    hd = vt_ref.shape[2] // HEADS_PER_STEP

    k_loc = lax.broadcasted_iota(jnp.int32, (tile, tile), 0)
    q_loc = lax.broadcasted_iota(jnp.int32, (tile, tile), 1)
    causal = k_loc <= q_loc

    qts, alibi = [], []
    for j in range(HEADS_PER_STEP):
        if moba:
            r = lax.broadcasted_iota(jnp.int32, qt_ref.shape[1:], 0)
            qt = jnp.where((r >= j * hd) & (r < (j + 1) * hd), qt_ref[0], jnp.zeros((), BF16))
            slope = slopes_ref[pair * HEADS_PER_STEP + j]
            alibi.append(-slope * (q_loc - k_loc).astype(F32))
            bias = _moba_gate_bias(kmean_ref[0, :, 0, :].astype(BF16), qt, own, slope)
            for n in range(bias.shape[0]):
                bias_ref[j, n] = bias[n:n + 1, :]
        else:
            qt = qt_ref[0, j * HEAD_PAD:(j + 1) * HEAD_PAD, :]
        qts.append(qt)

    def scores(j, n):
        start = pl.multiple_of(n * tile, tile)
        if moba:
            return _dot(k_ref[0, pl.ds(start, tile), :], qts[j]) + alibi[j]
        return _dot(k_ref[0, pl.ds(start, tile), j * HEAD_PAD:(j + 1) * HEAD_PAD], qts[j])

    def values(j, n):
        return vt_ref[0, n, j * hd:(j + 1) * hd, :]

    carry = []
    for j in range(HEADS_PER_STEP):
        s = jnp.where(causal, scores(j, own), MASKED)
        m = jnp.max(s, axis=0, keepdims=True)
        p = jnp.exp(s - m)
        carry += [m, jnp.sum(p, axis=0, keepdims=True), _dot(values(j, own), p.astype(BF16))]

    def past_block(n, carry):
        out = []
        for j in range(HEADS_PER_STEP):
            m, l, acc = carry[3 * j:3 * j + 3]
            s = scores(j, n)
            if moba:
                s = s + bias_ref[j, n]
            m_new = jnp.maximum(m, jnp.max(s, axis=0, keepdims=True))
            p = jnp.exp(s - m_new)
            a = jnp.exp(m - m_new)
            out += [m_new, a * l + jnp.sum(p, axis=0, keepdims=True),
                    a * acc + _dot(values(j, n), p.astype(BF16))]
        return tuple(out)

    carry = lax.fori_loop(0, own, past_block, tuple(carry))
    out_t = jnp.concatenate(
        [carry[3 * j + 2] / carry[3 * j + 1] for j in range(HEADS_PER_STEP)], axis=0)
    o_ref[0] = out_t.T.astype(o_ref.dtype)


def _attention(qt, k, vt, kmean=None):
    moba = kmean is not None
    B, _, S = qt.shape
    nb, v_w = vt.shape[1], vt.shape[2]
    heads = MOBA_HEADS if moba else MLA_HEADS
    pairs = heads // HEADS_PER_STEP
    tile = ATTN_TILE
    assert S % tile == 0 and vt.shape[3] == tile
    q_rows = qt.shape[1] // pairs
    k_cols = k.shape[2] // pairs
    v_rows = v_w // pairs

    in_specs = [
        pl.BlockSpec((1, q_rows, tile), lambda b, p, i, *_: (b, p, i)),
        pl.BlockSpec((1, S, k_cols), lambda b, p, i, *_: (b, 0, p)),
        pl.BlockSpec((1, nb, v_rows, tile), lambda b, p, i, *_: (b, 0, p, 0)),
    ]
    args = [qt, k, vt]
    scratch = []
    num_prefetch = 0
    if moba:
        in_specs.append(pl.BlockSpec((1, nb, 1, k_cols), lambda b, p, i, *_: (b, 0, 0, p)))
        slopes = 2.0 ** (-8.0 * jnp.arange(1, heads + 1, dtype=F32) / heads)
        args = [slopes] + args + [kmean]
        scratch = [pltpu.VMEM((HEADS_PER_STEP, nb, 1, tile), F32)]
        num_prefetch = 1
    grid_spec = pltpu.PrefetchScalarGridSpec(
        num_scalar_prefetch=num_prefetch,
        grid=(B, pairs, S // tile),
        in_specs=in_specs,
        out_specs=pl.BlockSpec((1, tile, v_rows), lambda b, p, i, *_: (b, i, p)),
        scratch_shapes=scratch,
    )
    return pl.pallas_call(
        functools.partial(_attn_kernel, moba=moba),
        grid_spec=grid_spec,
        out_shape=jax.ShapeDtypeStruct((B, S, v_w), BF16),
        compiler_params=pltpu.CompilerParams(
            dimension_semantics=("parallel", "parallel", "arbitrary"), vmem_limit_bytes=VMEM_LIMIT),
        name="moba_attention" if moba else "mla_attention",
    )(*args)


def _ffn_kernel(x_ref, a_ref, b_ref, woa_ref, wob_ref, g1_ref, b1_ref,
                wup_ref, cw_ref, cb_ref, wdn_ref, g2_ref, b2_ref,
                o_ref, act_ref, hbuf_ref, carry_ref, *, alpha):
    rows = x_ref.shape[1]
    d_ff = wdn_ref.shape[0]
    chunk = FFN_CHUNK

    @pl.when(pl.program_id(1) == 0)
    def _():
        carry_ref[...] = jnp.zeros(carry_ref.shape, F32)

    mix = _dot(a_ref[0], woa_ref[...]) + _dot(b_ref[0], wob_ref[...])
    x1 = _layer_norm(alpha * x_ref[0] + mix, g1_ref[...], b1_ref[...])
    x1b = x1.astype(BF16)

    def conv_branch(col, slot):
        cols = slice(col, col + chunk)
        h = _dot(x1b, wup_ref[:, cols])
        hbuf_ref[slot, 0:CARRY_ROWS, :] = carry_ref[:, cols]
        hbuf_ref[slot, CARRY_ROWS:CARRY_ROWS + rows, :] = h
        carry_ref[:, cols] = h[rows - CARRY_ROWS:, :]
        cw = cw_ref[:, cols]
        y = h * cw[CONV_W - 1:CONV_W, :] + cb_ref[:, cols]
        for tap in range(CONV_W - 1):
            back = CONV_W - 1 - tap
            y = y + hbuf_ref[slot, CARRY_ROWS - back:CARRY_ROWS - back + rows, :] * cw[tap:tap + 1, :]
        return y

    for c in range(d_ff // chunk):
        gate = conv_branch(c * chunk, 0)
        up = conv_branch(d_ff + c * chunk, 1)
        act = gate * (1.0 / (1.0 + jnp.exp(-gate))) * up
        act_ref[:, c * chunk:(c + 1) * chunk] = act.astype(BF16)

    y = _dot(act_ref[...], wdn_ref[...])
    o_ref[0] = _layer_norm(alpha * x1 + y, g2_ref[...], b2_ref[...])


def _mix_and_ffn(x, a_out, b_out, w_o, ln1_g, ln1_b, w_up, conv_w, conv_b, w_down, ln2_g, ln2_b, alpha):
    B, S, D = x.shape
    rows = FFN_ROWS
    d_ff = w_down.shape[0]
    assert S % rows == 0 and d_ff % FFN_CHUNK == 0 and conv_w.shape[0] == CONV_W
    a_w = a_out.shape[2]
    row_spec = lambda w: pl.BlockSpec((1, rows, w), lambda b, i: (b, i, 0))
    vec = lambda v: v.reshape(1, -1).astype(F32)
    in_specs = [
        row_spec(D), row_spec(a_w), row_spec(b_out.shape[2]),
        _const_spec((a_w, D)), _const_spec((w_o.shape[0] - a_w, D)),
        _const_spec((1, D)), _const_spec((1, D)),
        _const_spec(w_up.shape), _const_spec(conv_w.shape), _const_spec((1, 2 * d_ff)),
        _const_spec(w_down.shape), _const_spec((1, D)), _const_spec((1, D)),
    ]
    return pl.pallas_call(
        functools.partial(_ffn_kernel, alpha=alpha),
        grid=(B, S // rows),
        in_specs=in_specs,
        out_specs=row_spec(D),
        out_shape=jax.ShapeDtypeStruct((B, S, D), x.dtype),
        scratch_shapes=[
            pltpu.VMEM((rows, d_ff), BF16),
            pltpu.VMEM((2, CARRY_ROWS + rows, FFN_CHUNK), F32),
            pltpu.VMEM((CARRY_ROWS, 2 * d_ff), F32),
        ],
        compiler_params=pltpu.CompilerParams(
            dimension_semantics=("parallel", "arbitrary"), vmem_limit_bytes=VMEM_LIMIT),
        name="mix_ffn",
    )(x, a_out, b_out, w_o[:a_w].astype(BF16), w_o[a_w:].astype(BF16), vec(ln1_g), vec(ln1_b),
      w_up.astype(BF16), conv_w.astype(F32), vec(conv_b), w_down.astype(BF16), vec(ln2_g), vec(ln2_b))


def kernel(x, w_in, q_norm_g, w_uq, kv_norm_g, w_ukv, w_o, ln1_g, ln1_b, w_up, conv_w, conv_b, w_down, ln2_g, ln2_b):
    depth = w_in.shape[0]
    alpha = (2.0 * depth) ** 0.25
    for l in range(depth):
        qat, ka, vat, qmt, km, vmt, kmean = _input_projection(
            x, w_in[l], q_norm_g[l], w_uq[l], kv_norm_g[l], w_ukv[l])
        a_out = _attention(qat, ka, vat)
        b_out = _attention(qmt, km, vmt, kmean)
        x = _mix_and_ffn(x, a_out, b_out, w_o[l], ln1_g[l], ln1_b[l], w_up[l], conv_w[l], conv_b[l],
                         w_down[l], ln2_g[l], ln2_b[l], alpha)
    return x
```

```python
import functools
import math

import jax
import jax.numpy as jnp
from jax import lax
from jax.experimental import pallas as pl
from jax.experimental.pallas import tpu as pltpu

MLA_HEADS = 8
MLA_NOPE = 64
MLA_ROPE = 32
MLA_V = 64
ROPE_THETA = 10000.0
MOBA_HEADS = 8
MOBA_HD = 64
MOBA_BLOCK = 256
MOBA_TOPK = 3
CONV_W = 3
EPS = 1e-5

LANES = 128
HEAD_PAD = 128
HEADS_PER_GROUP = HEAD_PAD // MOBA_HD
ATTN_KEYS = MOBA_BLOCK
ATTN_QUERIES = 2 * ATTN_KEYS
SUM_ROWS = 16
MASKED = -1e30
LOG2E = math.log2(math.e)
PROJ_ROWS = 512
FFN_ROWS = 512
FFN_CHUNK = 256
CARRY_ROWS = 8
VMEM_LIMIT = 48 * 1024 * 1024

F32 = jnp.float32
BF16 = jnp.bfloat16

_dot = functools.partial(jnp.dot, preferred_element_type=F32)


def _rms_norm(x, g):
    ms = jnp.mean(x * x, axis=-1, keepdims=True)
    return x * lax.rsqrt(ms + EPS) * g


def _layer_norm(x, g, b):
    mu = jnp.mean(x, axis=-1, keepdims=True)
    xc = x - mu
    var = jnp.mean(xc * xc, axis=-1, keepdims=True)
    return xc * lax.rsqrt(var + EPS) * g + b


def _proj_kernel(x_ref, win_ref, qg_ref, wqa_ref, wqb_ref, kvg_ref, wk_ref, wv_ref,
                 tq1_ref, tq2_ref, tk1_ref, tk2_ref,
                 qat_ref, ka_ref, vat_ref, qmt_ref, km_ref, vmt_ref, kmean_ref):
    q_lora = wqa_ref.shape[0]
    kv_lora = wk_ref.shape[0]
    moba_w = km_ref.shape[2]
    rows = x_ref.shape[1]
    c0 = q_lora + kv_lora
    c1 = c0 + HEAD_PAD

    xb = x_ref[0].astype(BF16)
    c_q = _dot(xb, win_ref[:, 0:q_lora])
    c_kv = _dot(xb, win_ref[:, q_lora:c0])
    rope_grp = _dot(xb, win_ref[:, c0:c1])
    qm = _dot(xb, win_ref[:, c1:c1 + moba_w]) * (MOBA_HD ** -0.5)
    km = _dot(xb, win_ref[:, c1 + moba_w:c1 + 2 * moba_w])
    vm = _dot(xb, win_ref[:, c1 + 2 * moba_w:c1 + 3 * moba_w])

    cqn = _rms_norm(c_q, qg_ref[...]).astype(BF16)
    ckn = _rms_norm(c_kv, kvg_ref[...]).astype(BF16)
    qa = _dot(cqn, wqa_ref[...])
    qb = _dot(cqn, wqb_ref[...])
    kk = _dot(ckn, wk_ref[...])
    vv = _dot(ckn, wv_ref[...])

    k_pe = rope_grp * tk1_ref[...] + pltpu.roll(rope_grp, HEAD_PAD - MLA_ROPE, 1) * tk2_ref[...]
    tq1 = tq1_ref[...]
    tq2 = tq2_ref[...]
    for h in range(MLA_HEADS):
        sl = slice(h * HEAD_PAD, (h + 1) * HEAD_PAD)
        q_h = qa[:, sl] * tq1 + qb[:, sl] * tq2
        qat_ref[0, sl, :] = q_h.T.astype(BF16)
        ka_ref[0, :, sl] = (kk[:, sl] + k_pe).astype(BF16)

    qmt_ref[0] = qm.T.astype(BF16)
    km_ref[0] = (km * LOG2E).astype(BF16)
    for j in range(rows // MOBA_BLOCK):
        blk = slice(j * MOBA_BLOCK, (j + 1) * MOBA_BLOCK)
        vat_ref[0, j] = vv[blk, :].T.astype(BF16)
        vmt_ref[0, j] = vm[blk, :].T.astype(BF16)
        kmean_ref[0, j] = jnp.mean(km[blk, :], axis=0, keepdims=True)


def _rope_tables(seq):
    half = MLA_ROPE // 2
    inv = ROPE_THETA ** (-jnp.arange(half, dtype=F32) / half)
    ang = jnp.arange(seq, dtype=F32)[:, None] * inv[None, :]
    cos2 = jnp.concatenate([jnp.cos(ang), jnp.cos(ang)], axis=1)
    sin2 = jnp.concatenate([jnp.sin(ang), jnp.sin(ang)], axis=1)
    lo = jnp.zeros((seq, MLA_NOPE), F32)
    hi = jnp.zeros((seq, HEAD_PAD - MLA_NOPE - MLA_ROPE), F32)
    tk1 = jnp.concatenate([lo, cos2, hi], axis=1)
    tk2 = jnp.concatenate([lo, sin2, hi], axis=1)
    scale = (MLA_NOPE + MLA_ROPE) ** -0.5 * LOG2E
    tq1 = jnp.concatenate([lo + 1.0, cos2, hi], axis=1) * scale
    tq2 = tk2 * scale
    return tq1, tq2, tk1, tk2


def _rotate_half_cols(w):
    half = w.shape[-1] // 2
    return jnp.concatenate([-w[..., half:], w[..., :half]], axis=-1)


def _prep_proj_weights(w_in, w_uq, w_ukv):
    d_model = w_in.shape[0]
    q_lora = w_uq.shape[0]
    kv_lora = w_ukv.shape[0]
    moba_w = MOBA_HEADS * MOBA_HD
    o = q_lora + kv_lora
    k_rope = w_in[:, o:o + MLA_ROPE]
    rope_grp = jnp.concatenate(
        [jnp.zeros((d_model, MLA_NOPE), w_in.dtype), k_rope, _rotate_half_cols(k_rope)], axis=1)
    o += MLA_ROPE
    win = jnp.concatenate([w_in[:, :q_lora + kv_lora], rope_grp, w_in[:, o:o + 3 * moba_w]], axis=1)

    wq = w_uq.reshape(q_lora, MLA_HEADS, MLA_NOPE + MLA_ROPE)
    pad = jnp.zeros((q_lora, MLA_HEADS, HEAD_PAD - MLA_NOPE - MLA_ROPE), w_uq.dtype)
    wqa = jnp.concatenate([wq, pad], axis=-1).reshape(q_lora, MLA_HEADS * HEAD_PAD)
    wqb = jnp.concatenate([jnp.zeros((q_lora, MLA_HEADS, MLA_NOPE), w_uq.dtype),
                           _rotate_half_cols(wq[..., MLA_NOPE:]), pad], axis=-1)
    wqb = wqb.reshape(q_lora, MLA_HEADS * HEAD_PAD)

    wkv = w_ukv.reshape(kv_lora, MLA_HEADS, MLA_NOPE + MLA_V)
    wk = jnp.concatenate([wkv[..., :MLA_NOPE],
                          jnp.zeros((kv_lora, MLA_HEADS, HEAD_PAD - MLA_NOPE), w_ukv.dtype)], axis=-1)
    wk = wk.reshape(kv_lora, MLA_HEADS * HEAD_PAD)
    wv = wkv[..., MLA_NOPE:].reshape(kv_lora, MLA_HEADS * MLA_V)
    return win.astype(BF16), wqa.astype(BF16), wqb.astype(BF16), wk.astype(BF16), wv.astype(BF16)


def _const_spec(shape):
    return pl.BlockSpec(shape, lambda *_: (0,) * len(shape))


def _input_projection(x, w_in, q_norm_g, w_uq, kv_norm_g, w_ukv):
    B, S, D = x.shape
    rows = PROJ_ROWS
    assert S % rows == 0 and rows % MOBA_BLOCK == 0
    nb = S // MOBA_BLOCK
    win, wqa, wqb, wk, wv = _prep_proj_weights(w_in, w_uq, w_ukv)
    tables = _rope_tables(S)
    mla_w = MLA_HEADS * HEAD_PAD
    v_w = MLA_HEADS * MLA_V
    moba_w = MOBA_HEADS * MOBA_HD
    blocks_per_step = rows // MOBA_BLOCK

    table_spec = pl.BlockSpec((rows, HEAD_PAD), lambda b, i: (i, 0))
    in_specs = [
        pl.BlockSpec((1, rows, D), lambda b, i: (b, i, 0)),
        _const_spec(win.shape),
        _const_spec((1, q_norm_g.shape[-1])), _const_spec(wqa.shape), _const_spec(wqb.shape),
        _const_spec((1, kv_norm_g.shape[-1])), _const_spec(wk.shape), _const_spec(wv.shape),
        table_spec, table_spec, table_spec, table_spec,
    ]
    out_shape = [
        jax.ShapeDtypeStruct((B, mla_w, S), BF16),
        jax.ShapeDtypeStruct((B, S, mla_w), BF16),
        jax.ShapeDtypeStruct((B, nb, v_w, MOBA_BLOCK), BF16),
        jax.ShapeDtypeStruct((B, moba_w, S), BF16),
        jax.ShapeDtypeStruct((B, S, moba_w), BF16),
        jax.ShapeDtypeStruct((B, nb, moba_w, MOBA_BLOCK), BF16),
        jax.ShapeDtypeStruct((B, nb, 1, moba_w), F32),
    ]
    out_specs = [
        pl.BlockSpec((1, mla_w, rows), lambda b, i: (b, 0, i)),
        pl.BlockSpec((1, rows, mla_w), lambda b, i: (b, i, 0)),
        pl.BlockSpec((1, blocks_per_step, v_w, MOBA_BLOCK), lambda b, i: (b, i, 0, 0)),
        pl.BlockSpec((1, moba_w, rows), lambda b, i: (b, 0, i)),
        pl.BlockSpec((1, rows, moba_w), lambda b, i: (b, i, 0)),
        pl.BlockSpec((1, blocks_per_step, moba_w, MOBA_BLOCK), lambda b, i: (b, i, 0, 0)),
        pl.BlockSpec((1, blocks_per_step, 1, moba_w), lambda b, i: (b, i, 0, 0)),
    ]
    return pl.pallas_call(
        _proj_kernel,
        grid=(B, S // rows),
        in_specs=in_specs,
        out_specs=out_specs,
        out_shape=out_shape,
        compiler_params=pltpu.CompilerParams(
            dimension_semantics=("parallel", "parallel"), vmem_limit_bytes=VMEM_LIMIT),
        name="input_projection",
    )(x, win, q_norm_g.reshape(1, -1), wqa, wqb, kv_norm_g.reshape(1, -1), wk, wv, *tables)


def _moba_bias(kmean, qt, first, slope):
    nb, tq = kmean.shape[0], qt.shape[1]
    g = _dot(kmean, qt)
    blk = lax.broadcasted_iota(jnp.int32, g.shape, 0)
    own = first + lax.broadcasted_iota(jnp.int32, g.shape, 1) // MOBA_BLOCK
    past = blk < own
    g = jnp.where(past, g, -jnp.inf)
    rank = jnp.zeros(g.shape, F32)
    for m in range(nb):
        row = g[m:m + 1, :]
        beats = (row > g) | ((row == g) & (blk > m))
        rank = rank + jnp.where(beats, 1.0, 0.0)
    keep = (past & (rank < MOBA_TOPK)) | (blk == own)
    tile_dist = ((first - blk) * MOBA_BLOCK).astype(F32)
    return jnp.where(keep, 0.0, MASKED) - slope * tile_dist


def _attn_kernel(*refs, moba, heads):
    if moba:
        (slopes_ref, qt_ref, k_ref, vt_ref, kmean_ref, o_ref,
         m_ref, acc_ref, s_ref, qm_ref, alibi_ref, bias_ref) = refs
    else:
        qt_ref, k_ref, vt_ref, o_ref, m_ref, acc_ref, s_ref = refs
    tk, tq = ATTN_KEYS, ATTN_QUERIES
    first = pl.program_id(1) * (tq // tk)
    hd = vt_ref.shape[2] // heads
    qd = qt_ref.shape[1] // heads

    k_loc = lax.broadcasted_iota(jnp.int32, (tk, tq), 0)
    q_loc = lax.broadcasted_iota(jnp.int32, (tk, tq), 1)

    if moba:
        for h in range(heads):
            grp = h // HEADS_PER_GROUP
            rows = slice(grp * HEAD_PAD, (grp + 1) * HEAD_PAD)
            r = lax.broadcasted_iota(jnp.int32, (HEAD_PAD, tq), 0)
            lo = (h % HEADS_PER_GROUP) * qd
            qt = jnp.where((r >= lo) & (r < lo + qd), qt_ref[0, rows, :], jnp.zeros((), BF16))
            qm_ref[h] = qt
            slope = slopes_ref[h]
            alibi_ref[h] = -slope * (q_loc - k_loc).astype(F32)
            bias = _moba_bias(kmean_ref[0, :, 0, rows].astype(BF16), qt, first, slope)
            for n in range(bias.shape[0]):
                bias_ref[h, n] = bias[n:n + 1, :]

    def qk(h, n):
        start = pl.multiple_of(n * tk, tk)
        if moba:
            grp = h // HEADS_PER_GROUP
            return _dot(k_ref[0, pl.ds(start, tk), grp * HEAD_PAD:(grp + 1) * HEAD_PAD], qm_ref[h])
        cols = slice(h * HEAD_PAD, (h + 1) * HEAD_PAD)
        return _dot(k_ref[0, pl.ds(start, tk), cols], qt_ref[0, cols, :])

    ones = jnp.ones((SUM_ROWS, tk), BF16)

    def pv(h, n, p):
        return _dot(jnp.concatenate([vt_ref[0, n, h * hd:(h + 1) * hd, :], ones], axis=0), p)

    def diag_softmax_pv(h, d):
        n = first + d
        s = s_ref[h]
        if moba:
            s = s + alibi_ref[h] + bias_ref[h, n]
        s = jnp.where((k_loc + d * tk) <= q_loc, s, MASKED)
        m_tile = jnp.max(s, axis=0, keepdims=True)
        if d == 0:
            m_new = m_tile
            acc_ref[h] = pv(h, n, jnp.exp2((s - m_new).astype(BF16)))
        else:
            m_old = m_ref[h]
            m_new = jnp.maximum(m_old, m_tile)
            acc_ref[h] = (jnp.exp2(m_old - m_new) * acc_ref[h]
                          + pv(h, n, jnp.exp2((s - m_new).astype(BF16))))
        m_ref[h] = m_new

    def softmax_pv(h, n):
        s = s_ref[h]
        m_old = m_ref[h]
        if moba:
            s = s + alibi_ref[h]
            b = bias_ref[h, n]
            m_new = jnp.maximum(m_old, jnp.max(s, axis=0, keepdims=True) + b)
            p = jnp.exp2((s - (m_new - b)).astype(BF16))
        else:
            m_new = jnp.maximum(m_old, jnp.max(s, axis=0, keepdims=True))
            p = jnp.exp2((s - m_new).astype(BF16))
        m_ref[h] = m_new
        acc_ref[h] = jnp.exp2(m_old - m_new) * acc_ref[h] + pv(h, n, p)

    diag_blocks = tq // tk
    for h in range(heads):
        s_ref[h] = qk(h, first)
    for d in range(diag_blocks):
        for h in range(heads):
            diag_softmax_pv(h, d)
            s_ref[h] = qk(h, first + d + 1) if d + 1 < diag_blocks else qk(h, 0)

    def past_block(n, carry):
        for h in range(heads):
            softmax_pv(h, n)
            s_ref[h] = qk(h, n + 1)
        return carry

    lax.fori_loop(0, first - 1, past_block, 0)

    @pl.when(first > 0)
    def _():
        for h in range(heads):
            softmax_pv(h, first - 1)

    out_t = jnp.concatenate(
        [acc_ref[h, 0:hd, :] / acc_ref[h, hd:hd + 1, :] for h in range(heads)], axis=0)
    o_ref[0] = out_t.T.astype(o_ref.dtype)


def _attention(qt, k, vt, kmean=None):
    moba = kmean is not None
    B, q_w, S = qt.shape
    nb, v_w = vt.shape[1], vt.shape[2]
    heads = MOBA_HEADS if moba else MLA_HEADS
    tk, tq = ATTN_KEYS, ATTN_QUERIES
    assert S % tq == 0 and vt.shape[3] == tk
    k_w = k.shape[2]

    in_specs = [
        pl.BlockSpec((1, q_w, tq), lambda b, i, *_: (b, 0, i)),
        pl.BlockSpec((1, S, k_w), lambda b, i, *_: (b, 0, 0)),
        pl.BlockSpec((1, nb, v_w, tk), lambda b, i, *_: (b, 0, 0, 0)),
    ]
    args = [qt, k, vt]
    scratch = [
        pltpu.VMEM((heads, 1, tq), F32),
        pltpu.VMEM((heads, v_w // heads + SUM_ROWS, tq), F32),
        pltpu.VMEM((heads, tk, tq), F32),
    ]
    num_prefetch = 0
    if moba:
        in_specs.append(pl.BlockSpec((1, nb, 1, k_w), lambda b, i, *_: (b, 0, 0, 0)))
        slopes = LOG2E * 2.0 ** (-8.0 * jnp.arange(1, heads + 1, dtype=F32) / heads)
        args = [slopes] + args + [kmean]
        scratch += [
            pltpu.VMEM((heads, HEAD_PAD, tq), BF16),
            pltpu.VMEM((heads, tk, tq), F32),
            pltpu.VMEM((heads, nb, 1, tq), F32),
        ]
        num_prefetch = 1
    grid_spec = pltpu.PrefetchScalarGridSpec(
        num_scalar_prefetch=num_prefetch,
        grid=(B, S // tq),
        in_specs=in_specs,
        out_specs=pl.BlockSpec((1, tq, v_w), lambda b, i, *_: (b, i, 0)),
        scratch_shapes=scratch,
    )
    return pl.pallas_call(
        functools.partial(_attn_kernel, moba=moba, heads=heads),
        grid_spec=grid_spec,
        out_shape=jax.ShapeDtypeStruct((B, S, v_w), BF16),
        compiler_params=pltpu.CompilerParams(
            dimension_semantics=("parallel", "arbitrary"), vmem_limit_bytes=VMEM_LIMIT),
        name="moba_attention" if moba else "mla_attention",
    )(*args)


def _ffn_kernel(x_ref, a_ref, b_ref, woa_ref, wob_ref, g1_ref, b1_ref,
                wup_ref, cw_ref, cb_ref, wdn_ref, g2_ref, b2_ref,
                o_ref, act_ref, hbuf_ref, carry_ref, *, alpha):
    rows = x_ref.shape[1]
    d_ff = wdn_ref.shape[0]
    chunk = FFN_CHUNK

    @pl.when(pl.program_id(1) == 0)
    def _():
        carry_ref[...] = jnp.zeros(carry_ref.shape, F32)

    mix = _dot(a_ref[0], woa_ref[...]) + _dot(b_ref[0], wob_ref[...])
    x1 = _layer_norm(alpha * x_ref[0] + mix, g1_ref[...], b1_ref[...])
    x1b = x1.astype(BF16)

    def conv_branch(col, slot):
        cols = slice(col, col + chunk)
        h = _dot(x1b, wup_ref[:, cols])
        hbuf_ref[slot, 0:CARRY_ROWS, :] = carry_ref[:, cols]
        hbuf_ref[slot, CARRY_ROWS:CARRY_ROWS + rows, :] = h
        carry_ref[:, cols] = h[rows - CARRY_ROWS:, :]
        cw = cw_ref[:, cols]
        y = h * cw[CONV_W - 1:CONV_W, :] + cb_ref[:, cols]
        for tap in range(CONV_W - 1):
            back = CONV_W - 1 - tap
            y = y + hbuf_ref[slot, CARRY_ROWS - back:CARRY_ROWS - back + rows, :] * cw[tap:tap + 1, :]
        return y

    for c in range(d_ff // chunk):
        gate = conv_branch(c * chunk, 0)
        up = conv_branch(d_ff + c * chunk, 1)
        act = gate * (1.0 / (1.0 + jnp.exp(-gate))) * up
        act_ref[:, c * chunk:(c + 1) * chunk] = act.astype(BF16)

    y = _dot(act_ref[...], wdn_ref[...])
    o_ref[0] = _layer_norm(alpha * x1 + y, g2_ref[...], b2_ref[...])


def _mix_and_ffn(x, a_out, b_out, w_o, ln1_g, ln1_b, w_up, conv_w, conv_b, w_down, ln2_g, ln2_b, alpha):
    B, S, D = x.shape
    rows = FFN_ROWS
    d_ff = w_down.shape[0]
    assert S % rows == 0 and d_ff % FFN_CHUNK == 0 and conv_w.shape[0] == CONV_W
    a_w = a_out.shape[2]
    row_spec = lambda w: pl.BlockSpec((1, rows, w), lambda b, i: (b, i, 0))
    vec = lambda v: v.reshape(1, -1).astype(F32)
    in_specs = [
        row_spec(D), row_spec(a_w), row_spec(b_out.shape[2]),
        _const_spec((a_w, D)), _const_spec((w_o.shape[0] - a_w, D)),
        _const_spec((1, D)), _const_spec((1, D)),
        _const_spec(w_up.shape), _const_spec(conv_w.shape), _const_spec((1, 2 * d_ff)),
        _const_spec(w_down.shape), _const_spec((1, D)), _const_spec((1, D)),
    ]
    return pl.pallas_call(
        functools.partial(_ffn_kernel, alpha=alpha),
        grid=(B, S // rows),
        in_specs=in_specs,
        out_specs=row_spec(D),
        out_shape=jax.ShapeDtypeStruct((B, S, D), x.dtype),
        scratch_shapes=[
            pltpu.VMEM((rows, d_ff), BF16),
            pltpu.VMEM((2, CARRY_ROWS + rows, FFN_CHUNK), F32),
            pltpu.VMEM((CARRY_ROWS, 2 * d_ff), F32),
        ],
        compiler_params=pltpu.CompilerParams(
            dimension_semantics=("parallel", "arbitrary"), vmem_limit_bytes=VMEM_LIMIT),
        name="mix_ffn",
    )(x, a_out, b_out, w_o[:a_w].astype(BF16), w_o[a_w:].astype(BF16), vec(ln1_g), vec(ln1_b),
      w_up.astype(BF16), conv_w.astype(F32), vec(conv_b), w_down.astype(BF16), vec(ln2_g), vec(ln2_b))


def kernel(x, w_in, q_norm_g, w_uq, kv_norm_g, w_ukv, w_o, ln1_g, ln1_b, w_up, conv_w, conv_b, w_down, ln2_g, ln2_b):
    depth = w_in.shape[0]
    alpha = (2.0 * depth) ** 0.25
    for l in range(depth):
        qat, ka, vat, qmt, km, vmt, kmean = _input_projection(
            x, w_in[l], q_norm_g[l], w_uq[l], kv_norm_g[l], w_ukv[l])
        a_out = _attention(qat, ka, vat)
        b_out = _attention(qmt, km, vmt, kmean)
        x = _mix_and_ffn(x, a_out, b_out, w_o[l], ln1_g[l], ln1_b[l], w_up[l], conv_w[l], conv_b[l],
                         w_down[l], ln2_g[l], ln2_b[l], alpha)
    return x
```

```python
import functools
import math

import jax
import jax.numpy as jnp
from jax import lax
from jax.experimental import pallas as pl
from jax.experimental.pallas import tpu as pltpu

MLA_HEADS = 8
MLA_NOPE = 64
MLA_ROPE = 32
MLA_V = 64
ROPE_THETA = 10000.0
MOBA_HEADS = 8
MOBA_HD = 64
MOBA_BLOCK = 256
MOBA_TOPK = 3
CONV_W = 3
EPS = 1e-5

LANES = 128
HEAD_PAD = 128
HEADS_PER_GROUP = HEAD_PAD // MOBA_HD
BLOCKS_PER_TILE = 2
ATTN_TILE = BLOCKS_PER_TILE * MOBA_BLOCK
SLOPE_PIECES = 3
SUM_ROWS = 16
MASKED = -1e30
LOG2E = math.log2(math.e)
PROJ_ROWS = 512
FFN_ROWS = 512
FFN_PARTS = 2
FFN_CHUNK = 256
CARRY_ROWS = 8
VMEM_LIMIT = 48 * 1024 * 1024

F32 = jnp.float32
BF16 = jnp.bfloat16

_dot = functools.partial(jnp.dot, preferred_element_type=F32)


def _rms_norm(x, g):
    ms = jnp.mean(x * x, axis=-1, keepdims=True)
    return x * lax.rsqrt(ms + EPS) * g


def _layer_norm(x, g, b):
    mu = jnp.mean(x, axis=-1, keepdims=True)
    xc = x - mu
    var = jnp.mean(xc * xc, axis=-1, keepdims=True)
    return xc * lax.rsqrt(var + EPS) * g + b


def _proj_kernel(x_ref, win_ref, qg_ref, wqa_ref, kvg_ref, wk_ref, wv_ref,
                 tq1_ref, tqa_ref, tqb_ref, tk1_ref, tk2_ref,
                 qat_ref, ka_ref, vat_ref, qmt_ref, km_ref, vmt_ref, kmean_ref):
    q_lora = wqa_ref.shape[0]
    kv_lora = wk_ref.shape[0]
    moba_w = km_ref.shape[2]
    mla_w = wqa_ref.shape[1]
    rows = x_ref.shape[1]
    c0 = q_lora + kv_lora
    c1 = c0 + HEAD_PAD

    xb = x_ref[0].astype(BF16)
    c_q = _dot(xb, win_ref[:, 0:q_lora])
    kv_rope = _dot(xb, win_ref[:, q_lora:c1])
    c_kv = kv_rope[:, 0:kv_lora]
    rope_grp = kv_rope[:, kv_lora:kv_lora + HEAD_PAD]

    cqn = _rms_norm(c_q, qg_ref[...]).astype(BF16)
    ckn = _rms_norm(c_kv, kvg_ref[...]).astype(BF16)
    qa = _dot(cqn, wqa_ref[...])
    kk = _dot(ckn, wk_ref[...])
    vv = _dot(ckn, wv_ref[...])

    qm = _dot(xb, win_ref[:, c1:c1 + moba_w]) * (MOBA_HD ** -0.5)
    km = _dot(xb, win_ref[:, c1 + moba_w:c1 + 2 * moba_w])
    vm = _dot(xb, win_ref[:, c1 + 2 * moba_w:c1 + 3 * moba_w])

    half = MLA_ROPE // 2
    q_up = pltpu.roll(qa, mla_w - half, 1)
    q_dn = pltpu.roll(qa, half, 1)
    k_pe = rope_grp * tk1_ref[...] + pltpu.roll(rope_grp, HEAD_PAD - MLA_ROPE, 1) * tk2_ref[...]
    tq1 = tq1_ref[...]
    tqa = tqa_ref[...]
    tqb = tqb_ref[...]
    for h in range(MLA_HEADS):
        sl = slice(h * HEAD_PAD, (h + 1) * HEAD_PAD)
        q_h = qa[:, sl] * tq1 + q_up[:, sl] * tqa + q_dn[:, sl] * tqb
        qat_ref[0, sl, :] = q_h.T.astype(BF16)
        ka_ref[0, :, sl] = (kk[:, sl] + k_pe).astype(BF16)

    qmt_ref[0] = qm.T.astype(BF16)
    km_ref[0] = (km * LOG2E).astype(BF16)
    for j in range(rows // MOBA_BLOCK):
        blk = slice(j * MOBA_BLOCK, (j + 1) * MOBA_BLOCK)
        vat_ref[0, j] = vv[blk, :].T.astype(BF16)
        vmt_ref[0, j] = vm[blk, :].T.astype(BF16)
        kmean_ref[0, j] = jnp.mean(km[blk, :], axis=0, keepdims=True)


def _rope_tables(seq):
    half = MLA_ROPE // 2
    inv = ROPE_THETA ** (-jnp.arange(half, dtype=F32) / half)
    ang = jnp.arange(seq, dtype=F32)[:, None] * inv[None, :]
    cos, sin = jnp.cos(ang), jnp.sin(ang)
    z = lambda w: jnp.zeros((seq, w), F32)
    tail = HEAD_PAD - MLA_NOPE - MLA_ROPE
    tk1 = jnp.concatenate([z(MLA_NOPE), cos, cos, z(tail)], axis=1)
    tk2 = jnp.concatenate([z(MLA_NOPE), sin, sin, z(tail)], axis=1)
    scale = (MLA_NOPE + MLA_ROPE) ** -0.5 * LOG2E
    tq1 = jnp.concatenate([z(MLA_NOPE) + 1.0, cos, cos, z(tail)], axis=1) * scale
    tqa = jnp.concatenate([z(MLA_NOPE), -sin, z(half), z(tail)], axis=1) * scale
    tqb = jnp.concatenate([z(MLA_NOPE), z(half), sin, z(tail)], axis=1) * scale
    return tq1, tqa, tqb, tk1, tk2


def _rotate_half_cols(w):
    half = w.shape[-1] // 2
    return jnp.concatenate([-w[..., half:], w[..., :half]], axis=-1)


def _prep_proj_weights(w_in, w_uq, w_ukv):
    d_model = w_in.shape[0]
    q_lora = w_uq.shape[0]
    kv_lora = w_ukv.shape[0]
    moba_w = MOBA_HEADS * MOBA_HD
    o = q_lora + kv_lora
    k_rope = w_in[:, o:o + MLA_ROPE]
    rope_grp = jnp.concatenate(
        [jnp.zeros((d_model, MLA_NOPE), w_in.dtype), k_rope, _rotate_half_cols(k_rope)], axis=1)
    o += MLA_ROPE
    win = jnp.concatenate([w_in[:, :q_lora + kv_lora], rope_grp, w_in[:, o:o + 3 * moba_w]], axis=1)

    wq = w_uq.reshape(q_lora, MLA_HEADS, MLA_NOPE + MLA_ROPE)
    pad = jnp.zeros((q_lora, MLA_HEADS, HEAD_PAD - MLA_NOPE - MLA_ROPE), w_uq.dtype)
    wqa = jnp.concatenate([wq, pad], axis=-1).reshape(q_lora, MLA_HEADS * HEAD_PAD)

    wkv = w_ukv.reshape(kv_lora, MLA_HEADS, MLA_NOPE + MLA_V)
    wk = jnp.concatenate([wkv[..., :MLA_NOPE],
                          jnp.zeros((kv_lora, MLA_HEADS, HEAD_PAD - MLA_NOPE), w_ukv.dtype)], axis=-1)
    wk = wk.reshape(kv_lora, MLA_HEADS * HEAD_PAD)
    wv = wkv[..., MLA_NOPE:].reshape(kv_lora, MLA_HEADS * MLA_V)
    return win.astype(BF16), wqa.astype(BF16), wk.astype(BF16), wv.astype(BF16)


def _const_spec(shape):
    return pl.BlockSpec(shape, lambda *_: (0,) * len(shape))


def _input_projection(x, w_in, q_norm_g, w_uq, kv_norm_g, w_ukv):
    B, S, D = x.shape
    rows = PROJ_ROWS
    assert S % rows == 0 and rows % MOBA_BLOCK == 0
    nb = S // MOBA_BLOCK
    win, wqa, wk, wv = _prep_proj_weights(w_in, w_uq, w_ukv)
    tables = _rope_tables(S)
    mla_w = MLA_HEADS * HEAD_PAD
    v_w = MLA_HEADS * MLA_V
    moba_w = MOBA_HEADS * MOBA_HD
    blocks_per_step = rows // MOBA_BLOCK

    table_spec = pl.BlockSpec((rows, HEAD_PAD), lambda b, i: (i, 0))
    in_specs = [
        pl.BlockSpec((1, rows, D), lambda b, i: (b, i, 0)),
        _const_spec(win.shape),
        _const_spec((1, q_norm_g.shape[-1])), _const_spec(wqa.shape),
        _const_spec((1, kv_norm_g.shape[-1])), _const_spec(wk.shape), _const_spec(wv.shape),
    ] + [table_spec] * len(tables)
    out_shape = [
        jax.ShapeDtypeStruct((B, mla_w, S), BF16),
        jax.ShapeDtypeStruct((B, S, mla_w), BF16),
        jax.ShapeDtypeStruct((B, nb, v_w, MOBA_BLOCK), BF16),
        jax.ShapeDtypeStruct((B, moba_w, S), BF16),
        jax.ShapeDtypeStruct((B, S, moba_w), BF16),
        jax.ShapeDtypeStruct((B, nb, moba_w, MOBA_BLOCK), BF16),
        jax.ShapeDtypeStruct((B, nb, 1, moba_w), F32),
    ]
    out_specs = [
        pl.BlockSpec((1, mla_w, rows), lambda b, i: (b, 0, i)),
        pl.BlockSpec((1, rows, mla_w), lambda b, i: (b, i, 0)),
        pl.BlockSpec((1, blocks_per_step, v_w, MOBA_BLOCK), lambda b, i: (b, i, 0, 0)),
        pl.BlockSpec((1, moba_w, rows), lambda b, i: (b, 0, i)),
        pl.BlockSpec((1, rows, moba_w), lambda b, i: (b, i, 0)),
        pl.BlockSpec((1, blocks_per_step, moba_w, MOBA_BLOCK), lambda b, i: (b, i, 0, 0)),
        pl.BlockSpec((1, blocks_per_step, 1, moba_w), lambda b, i: (b, i, 0, 0)),
    ]
    return pl.pallas_call(
        _proj_kernel,
        grid=(B, S // rows),
        in_specs=in_specs,
        out_specs=out_specs,
        out_shape=out_shape,
        compiler_params=pltpu.CompilerParams(
            dimension_semantics=("parallel", "parallel"), vmem_limit_bytes=VMEM_LIMIT),
        name="input_projection",
    )(x, win, q_norm_g.reshape(1, -1), wqa, kv_norm_g.reshape(1, -1), wk, wv, *tables)


def _moba_bias(kmean, qt, tile, slope):
    nb = kmean.shape[0]
    g = _dot(kmean, qt)
    blk = lax.broadcasted_iota(jnp.int32, g.shape, 0)
    own = tile * BLOCKS_PER_TILE + lax.broadcasted_iota(jnp.int32, g.shape, 1) // MOBA_BLOCK
    past = blk < own
    g = jnp.where(past, g, -jnp.inf)
    rank = jnp.zeros(g.shape, F32)
    for m in range(nb):
        row = g[m:m + 1, :]
        beats = (row > g) | ((row == g) & (blk > m))
        rank = rank + jnp.where(beats, 1.0, 0.0)
    keep = (past & (rank < MOBA_TOPK)) | (blk == own)
    tile_dist = ((tile - blk // BLOCKS_PER_TILE) * ATTN_TILE).astype(F32)
    return jnp.where(keep, 0.0, MASKED) - slope * tile_dist


def _slope_table(heads):
    slopes = LOG2E * 2.0 ** (-8.0 * jnp.arange(1, heads + 1, dtype=F32) / heads)
    cols, rest = [slopes], slopes
    for _ in range(SLOPE_PIECES):
        piece = rest.astype(BF16).astype(F32)
        cols.append(piece)
        rest = rest - piece
    return jnp.stack(cols, axis=1).reshape(-1)


def _attn_kernel(*refs, moba, heads):
    if moba:
        (slopes_ref, qt_ref, k_ref, vt_ref, kmean_ref, o_ref,
         m_ref, acc_ref, s_ref, qm_ref, bias_ref) = refs
    else:
        qt_ref, k_ref, vt_ref, o_ref, m_ref, acc_ref, s_ref = refs
    tile = ATTN_TILE
    own = pl.program_id(1)
    hd = vt_ref.shape[2] // heads
    qd = qt_ref.shape[1] // heads

    k_loc = lax.broadcasted_iota(jnp.int32, (tile, tile), 0)
    q_loc = lax.broadcasted_iota(jnp.int32, (tile, tile), 1)
    causal = k_loc <= q_loc

    if moba:
        r = lax.broadcasted_iota(jnp.int32, (tile, HEAD_PAD), 0)
        lane = lax.broadcasted_iota(jnp.int32, (tile, HEAD_PAD), 1)
        row_hi = (r // MOBA_BLOCK) * MOBA_BLOCK
        key_pos = jnp.where(lane < SLOPE_PIECES, row_hi,
                            jnp.where(lane < 2 * SLOPE_PIECES, r - row_hi, 0)).astype(F32).astype(BF16)
        r2 = lax.broadcasted_iota(jnp.int32, (HEAD_PAD, tile), 0)
        for h in range(heads):
            grp = h // HEADS_PER_GROUP
            rows = slice(grp * HEAD_PAD, (grp + 1) * HEAD_PAD)
            lo = (h % HEADS_PER_GROUP) * qd
            qt = jnp.where((r2 >= lo) & (r2 < lo + qd), qt_ref[0, rows, :], jnp.zeros((), BF16))
            slope = slopes_ref[h * (SLOPE_PIECES + 1)]
            slope_rows = jnp.zeros((HEAD_PAD, tile), F32)
            for j in range(SLOPE_PIECES):
                piece = slopes_ref[h * (SLOPE_PIECES + 1) + 1 + j]
                slope_rows = jnp.where((r2 == j) | (r2 == SLOPE_PIECES + j), piece, slope_rows)
            qm_ref[h] = jnp.concatenate([qt, slope_rows.astype(BF16)], axis=0)
            bias = _moba_bias(kmean_ref[0, :, 0, rows].astype(BF16), qt, own, slope)
            for n in range(bias.shape[0]):
                bias_ref[h, n] = bias[n:n + 1, :]

    def qk(h, n):
        rows = pl.ds(pl.multiple_of(n * tile, tile), tile)
        if moba:
            grp = h // HEADS_PER_GROUP
            keys = k_ref[0, rows, grp * HEAD_PAD:(grp + 1) * HEAD_PAD]
            return _dot(jnp.concatenate([keys, key_pos], axis=1), qm_ref[h])
        cols = slice(h * HEAD_PAD, (h + 1) * HEAD_PAD)
        return _dot(k_ref[0, rows, cols], qt_ref[0, cols, :])

    ones = jnp.ones((SUM_ROWS, tile), BF16)

    def pv(h, n, p):
        vs = [vt_ref[0, n * BLOCKS_PER_TILE + j, h * hd:(h + 1) * hd, :] for j in range(BLOCKS_PER_TILE)]
        return _dot(jnp.concatenate([jnp.concatenate(vs, axis=1), ones], axis=0), p)

    def block_rows(j):
        return slice(j * MOBA_BLOCK, (j + 1) * MOBA_BLOCK)

    def diag_softmax_pv(h):
        s = s_ref[h]
        if moba:
            s = jnp.concatenate(
                [s[block_rows(j), :] + bias_ref[h, own * BLOCKS_PER_TILE + j] for j in range(BLOCKS_PER_TILE)],
                axis=0)
        s = jnp.where(causal, s, MASKED)
        m_new = jnp.max(s, axis=0, keepdims=True)
        m_ref[h] = m_new
        acc_ref[h] = pv(h, own, jnp.exp2((s - m_new).astype(BF16)))

    def softmax_pv(h, n):
        s = s_ref[h]
        m_old = m_ref[h]
        if moba:
            parts = [s[block_rows(j), :] for j in range(BLOCKS_PER_TILE)]
            biases = [bias_ref[h, n * BLOCKS_PER_TILE + j] for j in range(BLOCKS_PER_TILE)]
            m_new = m_old
            for part, b in zip(parts, biases):
                m_new = jnp.maximum(m_new, jnp.max(part, axis=0, keepdims=True) + b)
            p = jnp.concatenate(
                [jnp.exp2((part - (m_new - b)).astype(BF16)) for part, b in zip(parts, biases)], axis=0)
        else:
            m_new = jnp.maximum(m_old, jnp.max(s, axis=0, keepdims=True))
            p = jnp.exp2((s - m_new).astype(BF16))
        m_ref[h] = m_new
        acc_ref[h] = jnp.exp2(m_old - m_new) * acc_ref[h] + pv(h, n, p)

    for h in range(heads):
        s_ref[h] = qk(h, own)
    for h in range(heads):
        diag_softmax_pv(h)
        s_ref[h] = qk(h, 0)

    def past_tile(n, carry):
        for h in range(heads):
            softmax_pv(h, n)
            s_ref[h] = qk(h, n + 1)
        return carry

    lax.fori_loop(0, own - 1, past_tile, 0)

    @pl.when(own > 0)
    def _():
        for h in range(heads):
            softmax_pv(h, own - 1)

    out_t = jnp.concatenate(
        [acc_ref[h, 0:hd, :] / acc_ref[h, hd:hd + 1, :] for h in range(heads)], axis=0)
    o_ref[0] = out_t.T.astype(o_ref.dtype)


def _attention(qt, k, vt, kmean=None):
    moba = kmean is not None
    B, q_w, S = qt.shape
    nb, v_w = vt.shape[1], vt.shape[2]
    heads = MOBA_HEADS if moba else MLA_HEADS
    tile = ATTN_TILE
    assert S % tile == 0 and vt.shape[3] == MOBA_BLOCK
    k_w = k.shape[2]

    in_specs = [
        pl.BlockSpec((1, q_w, tile), lambda b, i, *_: (b, 0, i)),
        pl.BlockSpec((1, S, k_w), lambda b, i, *_: (b, 0, 0)),
        pl.BlockSpec((1, nb, v_w, MOBA_BLOCK), lambda b, i, *_: (b, 0, 0, 0)),
    ]
    args = [qt, k, vt]
    scratch = [
        pltpu.VMEM((heads, 1, tile), F32),
        pltpu.VMEM((heads, v_w // heads + SUM_ROWS, tile), F32),
        pltpu.VMEM((heads, tile, tile), F32),
    ]
    num_prefetch = 0
    if moba:
        in_specs.append(pl.BlockSpec((1, nb, 1, k_w), lambda b, i, *_: (b, 0, 0, 0)))
        args = [_slope_table(heads)] + args + [kmean]
        scratch += [
            pltpu.VMEM((heads, 2 * HEAD_PAD, tile), BF16),
            pltpu.VMEM((heads, nb, 1, tile), F32),
        ]
        num_prefetch = 1
    grid_spec = pltpu.PrefetchScalarGridSpec(
        num_scalar_prefetch=num_prefetch,
        grid=(B, S // tile),
        in_specs=in_specs,
        out_specs=pl.BlockSpec((1, tile, v_w), lambda b, i, *_: (b, i, 0)),
        scratch_shapes=scratch,
    )
    return pl.pallas_call(
        functools.partial(_attn_kernel, moba=moba, heads=heads),
        grid_spec=grid_spec,
        out_shape=jax.ShapeDtypeStruct((B, S, v_w), BF16),
        compiler_params=pltpu.CompilerParams(
            dimension_semantics=("parallel", "arbitrary"), vmem_limit_bytes=VMEM_LIMIT),
        name="moba_attention" if moba else "mla_attention",
    )(*args)


def _ffn_kernel(x_ref, a_ref, b_ref, woa_ref, wob_ref, g1_ref, b1_ref,
                wup_ref, cw_ref, cb_ref, wdn_ref, g2_ref, b2_ref,
                o_ref, act_ref, hbuf_ref, carry_ref, *, alpha):
    rows = x_ref.shape[1]
    d_ff = wdn_ref.shape[0]
    chunk = FFN_CHUNK

    @pl.when(pl.program_id(1) == 0)
    def _():
        carry_ref[...] = jnp.zeros(carry_ref.shape, F32)

    part_rows = rows // FFN_PARTS
    parts = [slice(p * part_rows, (p + 1) * part_rows) for p in range(FFN_PARTS)]

    mixes = [_dot(a_ref[0, r, :], woa_ref[...]) + _dot(b_ref[0, r, :], wob_ref[...]) for r in parts]
    x1 = [_layer_norm(alpha * x_ref[0, r, :] + mix, g1_ref[...], b1_ref[...]) for r, mix in zip(parts, mixes)]
    x1b = [v.astype(BF16) for v in x1]

    def conv_branch(p, col, branch, prev):
        cols = slice(col, col + chunk)
        slot = p * 2 + branch
        h = _dot(x1b[p], wup_ref[:, cols])
        hbuf_ref[slot, 0:CARRY_ROWS, :] = prev
        hbuf_ref[slot, CARRY_ROWS:CARRY_ROWS + part_rows, :] = h
        cw = cw_ref[:, cols]
        y = h * cw[CONV_W - 1:CONV_W, :] + cb_ref[:, cols]
        for tap in range(CONV_W - 1):
            back = CONV_W - 1 - tap
            shifted = hbuf_ref[slot, CARRY_ROWS - back:CARRY_ROWS - back + part_rows, :]
            y = y + shifted * cw[tap:tap + 1, :]
        return y, h[part_rows - CARRY_ROWS:, :]

    for c in range(d_ff // chunk):
        g_col, u_col = c * chunk, d_ff + c * chunk
        g_prev = carry_ref[:, g_col:g_col + chunk]
        u_prev = carry_ref[:, u_col:u_col + chunk]
        for p, r in enumerate(parts):
            gate, g_prev = conv_branch(p, g_col, 0, g_prev)
            up, u_prev = conv_branch(p, u_col, 1, u_prev)
            act = gate * (1.0 / (1.0 + jnp.exp(-gate))) * up
            act_ref[r, c * chunk:(c + 1) * chunk] = act.astype(BF16)
        carry_ref[:, g_col:g_col + chunk] = g_prev
        carry_ref[:, u_col:u_col + chunk] = u_prev

    ys = [_dot(act_ref[r, :], wdn_ref[...]) for r in parts]
    for r, v, y in zip(parts, x1, ys):
        o_ref[0, r, :] = _layer_norm(alpha * v + y, g2_ref[...], b2_ref[...])


def _mix_and_ffn(x, a_out, b_out, w_o, ln1_g, ln1_b, w_up, conv_w, conv_b, w_down, ln2_g, ln2_b, alpha):
    B, S, D = x.shape
    rows = FFN_ROWS
    d_ff = w_down.shape[0]
    assert S % rows == 0 and d_ff % FFN_CHUNK == 0 and conv_w.shape[0] == CONV_W
    a_w = a_out.shape[2]
    row_spec = lambda w: pl.BlockSpec((1, rows, w), lambda b, i: (b, i, 0))
    vec = lambda v: v.reshape(1, -1).astype(F32)
    in_specs = [
        row_spec(D), row_spec(a_w), row_spec(b_out.shape[2]),
        _const_spec((a_w, D)), _const_spec((w_o.shape[0] - a_w, D)),
        _const_spec((1, D)), _const_spec((1, D)),
        _const_spec(w_up.shape), _const_spec(conv_w.shape), _const_spec((1, 2 * d_ff)),
        _const_spec(w_down.shape), _const_spec((1, D)), _const_spec((1, D)),
    ]
    return pl.pallas_call(
        functools.partial(_ffn_kernel, alpha=alpha),
        grid=(B, S // rows),
        in_specs=in_specs,
        out_specs=row_spec(D),
        out_shape=jax.ShapeDtypeStruct((B, S, D), x.dtype),
        scratch_shapes=[
            pltpu.VMEM((rows, d_ff), BF16),
            pltpu.VMEM((2 * FFN_PARTS, CARRY_ROWS + rows // FFN_PARTS, FFN_CHUNK), F32),
            pltpu.VMEM((CARRY_ROWS, 2 * d_ff), F32),
        ],
        compiler_params=pltpu.CompilerParams(
            dimension_semantics=("parallel", "arbitrary"), vmem_limit_bytes=VMEM_LIMIT),
        name="mix_ffn",
    )(x, a_out, b_out, w_o[:a_w].astype(BF16), w_o[a_w:].astype(BF16), vec(ln1_g), vec(ln1_b),
      w_up.astype(BF16), conv_w.astype(F32), vec(conv_b), w_down.astype(BF16), vec(ln2_g), vec(ln2_b))


def kernel(x, w_in, q_norm_g, w_uq, kv_norm_g, w_ukv, w_o, ln1_g, ln1_b, w_up, conv_w, conv_b, w_down, ln2_g, ln2_b):
    depth = w_in.shape[0]
    alpha = (2.0 * depth) ** 0.25
    for l in range(depth):
        qat, ka, vat, qmt, km, vmt, kmean = _input_projection(
            x, w_in[l], q_norm_g[l], w_uq[l], kv_norm_g[l], w_ukv[l])
        a_out = _attention(qat, ka, vat)
        b_out = _attention(qmt, km, vmt, kmean)
        x = _mix_and_ffn(x, a_out, b_out, w_o[l], ln1_g[l], ln1_b[l], w_up[l], conv_w[l], conv_b[l],
                         w_down[l], ln2_g[l], ln2_b[l], alpha)
    return x
```

```python
import functools
import math

import jax
import jax.numpy as jnp
from jax import lax
from jax.experimental import pallas as pl
from jax.experimental.pallas import tpu as pltpu

MLA_HEADS = 8
MLA_NOPE = 64
MLA_ROPE = 32
MLA_V = 64
ROPE_THETA = 10000.0
MOBA_HEADS = 8
MOBA_HD = 64
MOBA_BLOCK = 256
MOBA_TOPK = 3
CONV_W = 3
EPS = 1e-5

LANES = 128
HEAD_PAD = 128
HEADS_PER_GROUP = HEAD_PAD // MOBA_HD
BLOCKS_PER_TILE = 2
ATTN_TILE = BLOCKS_PER_TILE * MOBA_BLOCK
SLOPE_PIECES = 3
SUM_ROWS = 16
MASKED = -1e30
LOG2E = math.log2(math.e)
PROJ_ROWS = 512
FFN_ROWS = 512
FFN_PARTS = 2
FFN_CHUNK = 256
CARRY_ROWS = 8
VMEM_LIMIT = 48 * 1024 * 1024

F32 = jnp.float32
BF16 = jnp.bfloat16

_dot = functools.partial(jnp.dot, preferred_element_type=F32)


def _rms_norm(x, g):
    ms = jnp.mean(x * x, axis=-1, keepdims=True)
    return x * lax.rsqrt(ms + EPS) * g


def _layer_norm(x, g, b):
    mu = jnp.mean(x, axis=-1, keepdims=True)
    xc = x - mu
    var = jnp.mean(xc * xc, axis=-1, keepdims=True)
    return xc * lax.rsqrt(var + EPS) * g + b


def _proj_kernel(x_ref, win_ref, qg_ref, wqa_ref, kvg_ref, wk_ref, wv_ref,
                 tq1_ref, tqa_ref, tqb_ref, tk1_ref, tk2_ref,
                 qat_ref, ka_ref, vat_ref, qmt_ref, km_ref, vmt_ref, kmean_ref):
    q_lora = wqa_ref.shape[0]
    kv_lora = wk_ref.shape[0]
    moba_w = km_ref.shape[2]
    mla_w = wqa_ref.shape[1]
    rows = x_ref.shape[1]
    c0 = q_lora + kv_lora
    c1 = c0 + HEAD_PAD

    xb = x_ref[0].astype(BF16)
    c_q = _dot(xb, win_ref[:, 0:q_lora])
    kv_rope = _dot(xb, win_ref[:, q_lora:c1])
    c_kv = kv_rope[:, 0:kv_lora]
    rope_grp = kv_rope[:, kv_lora:kv_lora + HEAD_PAD]

    cqn = _rms_norm(c_q, qg_ref[...]).astype(BF16)
    ckn = _rms_norm(c_kv, kvg_ref[...]).astype(BF16)
    qa = _dot(cqn, wqa_ref[...])
    kk = _dot(ckn, wk_ref[...])
    vv = _dot(ckn, wv_ref[...])

    qm = _dot(xb, win_ref[:, c1:c1 + moba_w]) * (MOBA_HD ** -0.5)
    km = _dot(xb, win_ref[:, c1 + moba_w:c1 + 2 * moba_w])
    vm = _dot(xb, win_ref[:, c1 + 2 * moba_w:c1 + 3 * moba_w])

    half = MLA_ROPE // 2
    q_up = pltpu.roll(qa, mla_w - half, 1)
    q_dn = pltpu.roll(qa, half, 1)
    k_pe = rope_grp * tk1_ref[...] + pltpu.roll(rope_grp, HEAD_PAD - MLA_ROPE, 1) * tk2_ref[...]
    tq1 = tq1_ref[...]
    tqa = tqa_ref[...]
    tqb = tqb_ref[...]
    for h in range(MLA_HEADS):
        sl = slice(h * HEAD_PAD, (h + 1) * HEAD_PAD)
        q_h = qa[:, sl] * tq1 + q_up[:, sl] * tqa + q_dn[:, sl] * tqb
        qat_ref[0, sl, :] = q_h.T.astype(BF16)
        ka_ref[0, :, sl] = (kk[:, sl] + k_pe).astype(BF16)

    qmt_ref[0] = qm.T.astype(BF16)
    km_ref[0] = (km * LOG2E).astype(BF16)
    for j in range(rows // MOBA_BLOCK):
        blk = slice(j * MOBA_BLOCK, (j + 1) * MOBA_BLOCK)
        vat_ref[0, j] = vv[blk, :].T.astype(BF16)
        vmt_ref[0, j] = vm[blk, :].T.astype(BF16)
        kmean_ref[0, j] = jnp.mean(km[blk, :], axis=0, keepdims=True)


def _rope_tables(seq):
    half = MLA_ROPE // 2
    inv = ROPE_THETA ** (-jnp.arange(half, dtype=F32) / half)
    ang = jnp.arange(seq, dtype=F32)[:, None] * inv[None, :]
    cos, sin = jnp.cos(ang), jnp.sin(ang)
    z = lambda w: jnp.zeros((seq, w), F32)
    tail = HEAD_PAD - MLA_NOPE - MLA_ROPE
    tk1 = jnp.concatenate([z(MLA_NOPE), cos, cos, z(tail)], axis=1)
    tk2 = jnp.concatenate([z(MLA_NOPE), sin, sin, z(tail)], axis=1)
    scale = (MLA_NOPE + MLA_ROPE) ** -0.5 * LOG2E
    tq1 = jnp.concatenate([z(MLA_NOPE) + 1.0, cos, cos, z(tail)], axis=1) * scale
    tqa = jnp.concatenate([z(MLA_NOPE), -sin, z(half), z(tail)], axis=1) * scale
    tqb = jnp.concatenate([z(MLA_NOPE), z(half), sin, z(tail)], axis=1) * scale
    return tq1, tqa, tqb, tk1, tk2


def _rotate_half_cols(w):
    half = w.shape[-1] // 2
    return jnp.concatenate([-w[..., half:], w[..., :half]], axis=-1)


def _prep_proj_weights(w_in, w_uq, w_ukv):
    d_model = w_in.shape[0]
    q_lora = w_uq.shape[0]
    kv_lora = w_ukv.shape[0]
    moba_w = MOBA_HEADS * MOBA_HD
    o = q_lora + kv_lora
    k_rope = w_in[:, o:o + MLA_ROPE]
    rope_grp = jnp.concatenate(
        [jnp.zeros((d_model, MLA_NOPE), w_in.dtype), k_rope, _rotate_half_cols(k_rope)], axis=1)
    o += MLA_ROPE
    win = jnp.concatenate([w_in[:, :q_lora + kv_lora], rope_grp, w_in[:, o:o + 3 * moba_w]], axis=1)

    wq = w_uq.reshape(q_lora, MLA_HEADS, MLA_NOPE + MLA_ROPE)
    pad = jnp.zeros((q_lora, MLA_HEADS, HEAD_PAD - MLA_NOPE - MLA_ROPE), w_uq.dtype)
    wqa = jnp.concatenate([wq, pad], axis=-1).reshape(q_lora, MLA_HEADS * HEAD_PAD)

    wkv = w_ukv.reshape(kv_lora, MLA_HEADS, MLA_NOPE + MLA_V)
    wk = jnp.concatenate([wkv[..., :MLA_NOPE],
                          jnp.zeros((kv_lora, MLA_HEADS, HEAD_PAD - MLA_NOPE), w_ukv.dtype)], axis=-1)
    wk = wk.reshape(kv_lora, MLA_HEADS * HEAD_PAD)
    wv = wkv[..., MLA_NOPE:].reshape(kv_lora, MLA_HEADS * MLA_V)
    return win.astype(BF16), wqa.astype(BF16), wk.astype(BF16), wv.astype(BF16)


def _const_spec(shape):
    return pl.BlockSpec(shape, lambda *_: (0,) * len(shape))


def _input_projection(x, w_in, q_norm_g, w_uq, kv_norm_g, w_ukv):
    B, S, D = x.shape
    rows = PROJ_ROWS
    assert S % rows == 0 and rows % MOBA_BLOCK == 0
    nb = S // MOBA_BLOCK
    win, wqa, wk, wv = _prep_proj_weights(w_in, w_uq, w_ukv)
    tables = _rope_tables(S)
    mla_w = MLA_HEADS * HEAD_PAD
    v_w = MLA_HEADS * MLA_V
    moba_w = MOBA_HEADS * MOBA_HD
    blocks_per_step = rows // MOBA_BLOCK

    table_spec = pl.BlockSpec((rows, HEAD_PAD), lambda b, i: (i, 0))
    in_specs = [
        pl.BlockSpec((1, rows, D), lambda b, i: (b, i, 0)),
        _const_spec(win.shape),
        _const_spec((1, q_norm_g.shape[-1])), _const_spec(wqa.shape),
        _const_spec((1, kv_norm_g.shape[-1])), _const_spec(wk.shape), _const_spec(wv.shape),
    ] + [table_spec] * len(tables)
    out_shape = [
        jax.ShapeDtypeStruct((B, mla_w, S), BF16),
        jax.ShapeDtypeStruct((B, S, mla_w), BF16),
        jax.ShapeDtypeStruct((B, nb, v_w, MOBA_BLOCK), BF16),
        jax.ShapeDtypeStruct((B, moba_w, S), BF16),
        jax.ShapeDtypeStruct((B, S, moba_w), BF16),
        jax.ShapeDtypeStruct((B, nb, moba_w, MOBA_BLOCK), BF16),
        jax.ShapeDtypeStruct((B, nb, 1, moba_w), F32),
    ]
    out_specs = [
        pl.BlockSpec((1, mla_w, rows), lambda b, i: (b, 0, i)),
        pl.BlockSpec((1, rows, mla_w), lambda b, i: (b, i, 0)),
        pl.BlockSpec((1, blocks_per_step, v_w, MOBA_BLOCK), lambda b, i: (b, i, 0, 0)),
        pl.BlockSpec((1, moba_w, rows), lambda b, i: (b, 0, i)),
        pl.BlockSpec((1, rows, moba_w), lambda b, i: (b, i, 0)),
        pl.BlockSpec((1, blocks_per_step, moba_w, MOBA_BLOCK), lambda b, i: (b, i, 0, 0)),
        pl.BlockSpec((1, blocks_per_step, 1, moba_w), lambda b, i: (b, i, 0, 0)),
    ]
    return pl.pallas_call(
        _proj_kernel,
        grid=(B, S // rows),
        in_specs=in_specs,
        out_specs=out_specs,
        out_shape=out_shape,
        compiler_params=pltpu.CompilerParams(
            dimension_semantics=("parallel", "parallel"), vmem_limit_bytes=VMEM_LIMIT),
        name="input_projection",
    )(x, win, q_norm_g.reshape(1, -1), wqa, kv_norm_g.reshape(1, -1), wk, wv, *tables)


def _moba_bias(kmean, qt, tile, slope):
    nb = kmean.shape[0]
    g = _dot(kmean, qt)
    blk = lax.broadcasted_iota(jnp.int32, g.shape, 0)
    blk_f = blk.astype(F32)
    own = tile * BLOCKS_PER_TILE + lax.broadcasted_iota(jnp.int32, g.shape, 1) // MOBA_BLOCK
    past = blk < own
    g = jnp.where(past, g, -jnp.inf)
    picked = blk < 0
    for _ in range(MOBA_TOPK):
        top = jnp.max(g, axis=0, keepdims=True)
        first = jnp.min(jnp.where(g == top, blk_f, float(nb)), axis=0, keepdims=True)
        hit = blk_f == first
        picked = picked | hit
        g = jnp.where(hit, -jnp.inf, g)
    keep = (picked & past) | (blk == own)
    tile_dist = ((tile - blk // BLOCKS_PER_TILE) * ATTN_TILE).astype(F32)
    return jnp.where(keep, 0.0, MASKED) - slope * tile_dist


def _slope_table(heads):
    slopes = LOG2E * 2.0 ** (-8.0 * jnp.arange(1, heads + 1, dtype=F32) / heads)
    cols, rest = [slopes], slopes
    for _ in range(SLOPE_PIECES):
        piece = rest.astype(BF16).astype(F32)
        cols.append(piece)
        rest = rest - piece
    return jnp.stack(cols, axis=1).reshape(-1)


def _attn_kernel(*refs, moba, heads):
    if moba:
        (slopes_ref, qt_ref, k_ref, vt_ref, kmean_ref, o_ref,
         m_ref, acc_ref, s_ref, smax_ref, qm_ref, bias_ref) = refs
    else:
        qt_ref, k_ref, vt_ref, o_ref, m_ref, acc_ref, s_ref, smax_ref = refs
    tile, blk = ATTN_TILE, MOBA_BLOCK
    assert BLOCKS_PER_TILE == 2
    own = pl.program_id(1)
    own_blk = own * BLOCKS_PER_TILE
    hd = vt_ref.shape[2] // heads
    qd = qt_ref.shape[1] // heads
    lo, hi = slice(0, blk), slice(blk, tile)

    causal = (lax.broadcasted_iota(jnp.int32, (blk, blk), 0)
              <= lax.broadcasted_iota(jnp.int32, (blk, blk), 1))

    if moba:
        r = lax.broadcasted_iota(jnp.int32, (tile, HEAD_PAD), 0)
        lane = lax.broadcasted_iota(jnp.int32, (tile, HEAD_PAD), 1)
        row_hi = (r // blk) * blk
        key_pos = jnp.where(lane < SLOPE_PIECES, row_hi,
                            jnp.where(lane < 2 * SLOPE_PIECES, r - row_hi, 0)).astype(F32).astype(BF16)
        r2 = lax.broadcasted_iota(jnp.int32, (HEAD_PAD, tile), 0)
        for h in range(heads):
            grp = h // HEADS_PER_GROUP
            rows = slice(grp * HEAD_PAD, (grp + 1) * HEAD_PAD)
            first = (h % HEADS_PER_GROUP) * qd
            qt = jnp.where((r2 >= first) & (r2 < first + qd), qt_ref[0, rows, :], jnp.zeros((), BF16))
            slope = slopes_ref[h * (SLOPE_PIECES + 1)]
            slope_rows = jnp.zeros((HEAD_PAD, tile), F32)
            for j in range(SLOPE_PIECES):
                piece = slopes_ref[h * (SLOPE_PIECES + 1) + 1 + j]
                slope_rows = jnp.where((r2 == j) | (r2 == SLOPE_PIECES + j), piece, slope_rows)
            qm_ref[h] = jnp.concatenate([qt, slope_rows.astype(BF16)], axis=0)
            bias = _moba_bias(kmean_ref[0, :, 0, rows].astype(BF16), qt, own, slope)
            for n in range(bias.shape[0]):
                bias_ref[h, n] = bias[n:n + 1, :]

    def keys(h, start, size, pos_rows):
        rows = pl.ds(pl.multiple_of(start, size), size)
        if moba:
            grp = h // HEADS_PER_GROUP
            return jnp.concatenate(
                [k_ref[0, rows, grp * HEAD_PAD:(grp + 1) * HEAD_PAD], key_pos[pos_rows, :]], axis=1)
        return k_ref[0, rows, h * HEAD_PAD:(h + 1) * HEAD_PAD]

    def queries(h):
        return qm_ref[h] if moba else qt_ref[0, h * HEAD_PAD:(h + 1) * HEAD_PAD, :]

    ones = jnp.ones((SUM_ROWS, tile), BF16)

    def values(h, first_block, blocks):
        vs = [vt_ref[0, first_block + j, h * hd:(h + 1) * hd, :] for j in range(blocks)]
        v = vs[0] if blocks == 1 else jnp.concatenate(vs, axis=1)
        return jnp.concatenate([v, ones[:, 0:blocks * blk]], axis=0)

    def issue_diag_scores(h):
        q = queries(h)
        s_ref[h, lo, :] = _dot(keys(h, own * tile, blk, lo), q)
        s_ref[h, hi, hi] = _dot(keys(h, own * tile + blk, blk, hi), q[:, hi])

    def diag_softmax_pv(h):
        s_ll, s_lh, s_hh = s_ref[h, lo, lo], s_ref[h, lo, hi], s_ref[h, hi, hi]
        if moba:
            b_lo, b_hi = bias_ref[h, own_blk], bias_ref[h, own_blk + 1]
            s_ll, s_lh, s_hh = s_ll + b_lo[:, lo], s_lh + b_lo[:, hi], s_hh + b_hi[:, hi]
        s_ll = jnp.where(causal, s_ll, MASKED)
        s_hh = jnp.where(causal, s_hh, MASKED)
        m_l = jnp.max(s_ll, axis=0, keepdims=True)
        m_h = jnp.maximum(jnp.max(s_lh, axis=0, keepdims=True), jnp.max(s_hh, axis=0, keepdims=True))
        p_l = jnp.exp2((s_ll - m_l).astype(BF16))
        p_h = jnp.concatenate([jnp.exp2((s_lh - m_h).astype(BF16)), jnp.exp2((s_hh - m_h).astype(BF16))], axis=0)
        m_ref[h] = jnp.concatenate([m_l, m_h], axis=1)
        acc_ref[h] = jnp.concatenate(
            [_dot(values(h, own_blk, 1), p_l), _dot(values(h, own_blk, BLOCKS_PER_TILE), p_h)], axis=1)

    def issue_scores(h, n):
        s = _dot(keys(h, n * tile, tile, slice(0, tile)), queries(h))
        s_ref[h] = s
        smax_ref[h, 0] = jnp.max(s[lo, :], axis=0, keepdims=True)
        smax_ref[h, 1] = jnp.max(s[hi, :], axis=0, keepdims=True)

    def softmax_pv(h, n):
        s = s_ref[h]
        m_old = m_ref[h]
        if moba:
            b_lo, b_hi = bias_ref[h, n * BLOCKS_PER_TILE], bias_ref[h, n * BLOCKS_PER_TILE + 1]
            m_new = jnp.maximum(m_old, jnp.maximum(smax_ref[h, 0] + b_lo, smax_ref[h, 1] + b_hi))
            p = jnp.concatenate([jnp.exp2((s[lo, :] - (m_new - b_lo)).astype(BF16)),
                                 jnp.exp2((s[hi, :] - (m_new - b_hi)).astype(BF16))], axis=0)
        else:
            m_new = jnp.maximum(m_old, jnp.maximum(smax_ref[h, 0], smax_ref[h, 1]))
            p = jnp.exp2((s - m_new).astype(BF16))
        m_ref[h] = m_new
        acc_ref[h] = jnp.exp2(m_old - m_new) * acc_ref[h] + _dot(values(h, n * BLOCKS_PER_TILE, BLOCKS_PER_TILE), p)

    for h in range(heads):
        issue_diag_scores(h)
    for h in range(heads):
        diag_softmax_pv(h)
        issue_scores(h, 0)

    def past_tile(n, carry):
        for h in range(heads):
            softmax_pv(h, n)
            issue_scores(h, n + 1)
        return carry

    lax.fori_loop(0, own - 1, past_tile, 0)

    @pl.when(own > 0)
    def _():
        for h in range(heads):
            softmax_pv(h, own - 1)

    out_t = jnp.concatenate(
        [acc_ref[h, 0:hd, :] / acc_ref[h, hd:hd + 1, :] for h in range(heads)], axis=0)
    o_ref[0] = out_t.T.astype(o_ref.dtype)


def _attention(qt, k, vt, kmean=None):
    moba = kmean is not None
    B, q_w, S = qt.shape
    nb, v_w = vt.shape[1], vt.shape[2]
    heads = MOBA_HEADS if moba else MLA_HEADS
    tile = ATTN_TILE
    assert S % tile == 0 and vt.shape[3] == MOBA_BLOCK
    k_w = k.shape[2]

    in_specs = [
        pl.BlockSpec((1, q_w, tile), lambda b, i, *_: (b, 0, i)),
        pl.BlockSpec((1, S, k_w), lambda b, i, *_: (b, 0, 0)),
        pl.BlockSpec((1, nb, v_w, MOBA_BLOCK), lambda b, i, *_: (b, 0, 0, 0)),
    ]
    args = [qt, k, vt]
    scratch = [
        pltpu.VMEM((heads, 1, tile), F32),
        pltpu.VMEM((heads, v_w // heads + SUM_ROWS, tile), F32),
        pltpu.VMEM((heads, tile, tile), F32),
        pltpu.VMEM((heads, BLOCKS_PER_TILE, 1, tile), F32),
    ]
    num_prefetch = 0
    if moba:
        in_specs.append(pl.BlockSpec((1, nb, 1, k_w), lambda b, i, *_: (b, 0, 0, 0)))
        args = [_slope_table(heads)] + args + [kmean]
        scratch += [
            pltpu.VMEM((heads, 2 * HEAD_PAD, tile), BF16),
            pltpu.VMEM((heads, nb, 1, tile), F32),
        ]
        num_prefetch = 1
    grid_spec = pltpu.PrefetchScalarGridSpec(
        num_scalar_prefetch=num_prefetch,
        grid=(B, S // tile),
        in_specs=in_specs,
        out_specs=pl.BlockSpec((1, tile, v_w), lambda b, i, *_: (b, i, 0)),
        scratch_shapes=scratch,
    )
    return pl.pallas_call(
        functools.partial(_attn_kernel, moba=moba, heads=heads),
        grid_spec=grid_spec,
        out_shape=jax.ShapeDtypeStruct((B, S, v_w), BF16),
        compiler_params=pltpu.CompilerParams(
            dimension_semantics=("parallel", "arbitrary"), vmem_limit_bytes=VMEM_LIMIT),
        name="moba_attention" if moba else "mla_attention",
    )(*args)


def _ffn_kernel(x_ref, a_ref, b_ref, woa_ref, wob_ref, g1_ref, b1_ref,
                wup_ref, cw_ref, cb_ref, wdn_ref, g2_ref, b2_ref,
                o_ref, act_ref, hbuf_ref, carry_ref, *, alpha):
    rows = x_ref.shape[1]
    d_ff = wdn_ref.shape[0]
    chunk = FFN_CHUNK

    @pl.when(pl.program_id(1) == 0)
    def _():
        carry_ref[...] = jnp.zeros(carry_ref.shape, F32)

    part_rows = rows // FFN_PARTS
    parts = [slice(p * part_rows, (p + 1) * part_rows) for p in range(FFN_PARTS)]

    mixes = [_dot(a_ref[0, r, :], woa_ref[...]) + _dot(b_ref[0, r, :], wob_ref[...]) for r in parts]
    x1 = [_layer_norm(alpha * x_ref[0, r, :] + mix, g1_ref[...], b1_ref[...]) for r, mix in zip(parts, mixes)]
    x1b = [v.astype(BF16) for v in x1]

    def conv_branch(p, col, branch, prev):
        cols = slice(col, col + chunk)
        slot = p * 2 + branch
        h = _dot(x1b[p], wup_ref[:, cols])
        hbuf_ref[slot, 0:CARRY_ROWS, :] = prev
        hbuf_ref[slot, CARRY_ROWS:CARRY_ROWS + part_rows, :] = h
        cw = cw_ref[:, cols]
        y = h * cw[CONV_W - 1:CONV_W, :] + cb_ref[:, cols]
        for tap in range(CONV_W - 1):
            back = CONV_W - 1 - tap
            shifted = hbuf_ref[slot, CARRY_ROWS - back:CARRY_ROWS - back + part_rows, :]
            y = y + shifted * cw[tap:tap + 1, :]
        return y, h[part_rows - CARRY_ROWS:, :]

    for c in range(d_ff // chunk):
        g_col, u_col = c * chunk, d_ff + c * chunk
        g_prev = carry_ref[:, g_col:g_col + chunk]
        u_prev = carry_ref[:, u_col:u_col + chunk]
        for p, r in enumerate(parts):
            gate, g_prev = conv_branch(p, g_col, 0, g_prev)
            up, u_prev = conv_branch(p, u_col, 1, u_prev)
            act = gate * (1.0 / (1.0 + jnp.exp(-gate))) * up
            act_ref[r, c * chunk:(c + 1) * chunk] = act.astype(BF16)
        carry_ref[:, g_col:g_col + chunk] = g_prev
        carry_ref[:, u_col:u_col + chunk] = u_prev

    ys = [_dot(act_ref[r, :], wdn_ref[...]) for r in parts]
    for r, v, y in zip(parts, x1, ys):
        o_ref[0, r, :] = _layer_norm(alpha * v + y, g2_ref[...], b2_ref[...])


def _mix_and_ffn(x, a_out, b_out, w_o, ln1_g, ln1_b, w_up, conv_w, conv_b, w_down, ln2_g, ln2_b, alpha):
    B, S, D = x.shape
    rows = FFN_ROWS
    d_ff = w_down.shape[0]
    assert S % rows == 0 and d_ff % FFN_CHUNK == 0 and conv_w.shape[0] == CONV_W
    a_w = a_out.shape[2]
    row_spec = lambda w: pl.BlockSpec((1, rows, w), lambda b, i: (b, i, 0))
    vec = lambda v: v.reshape(1, -1).astype(F32)
    in_specs = [
        row_spec(D), row_spec(a_w), row_spec(b_out.shape[2]),
        _const_spec((a_w, D)), _const_spec((w_o.shape[0] - a_w, D)),
        _const_spec((1, D)), _const_spec((1, D)),
        _const_spec(w_up.shape), _const_spec(conv_w.shape), _const_spec((1, 2 * d_ff)),
        _const_spec(w_down.shape), _const_spec((1, D)), _const_spec((1, D)),
    ]
    return pl.pallas_call(
        functools.partial(_ffn_kernel, alpha=alpha),
        grid=(B, S // rows),
        in_specs=in_specs,
        out_specs=row_spec(D),
        out_shape=jax.ShapeDtypeStruct((B, S, D), x.dtype),
        scratch_shapes=[
            pltpu.VMEM((rows, d_ff), BF16),
            pltpu.VMEM((2 * FFN_PARTS, CARRY_ROWS + rows // FFN_PARTS, FFN_CHUNK), F32),
            pltpu.VMEM((CARRY_ROWS, 2 * d_ff), F32),
        ],
        compiler_params=pltpu.CompilerParams(
            dimension_semantics=("parallel", "arbitrary"), vmem_limit_bytes=VMEM_LIMIT),
        name="mix_ffn",
    )(x, a_out, b_out, w_o[:a_w].astype(BF16), w_o[a_w:].astype(BF16), vec(ln1_g), vec(ln1_b),
      w_up.astype(BF16), conv_w.astype(F32), vec(conv_b), w_down.astype(BF16), vec(ln2_g), vec(ln2_b))


def kernel(x, w_in, q_norm_g, w_uq, kv_norm_g, w_ukv, w_o, ln1_g, ln1_b, w_up, conv_w, conv_b, w_down, ln2_g, ln2_b):
    depth = w_in.shape[0]
    alpha = (2.0 * depth) ** 0.25
    for l in range(depth):
        qat, ka, vat, qmt, km, vmt, kmean = _input_projection(
            x, w_in[l], q_norm_g[l], w_uq[l], kv_norm_g[l], w_ukv[l])
        a_out = _attention(qat, ka, vat)
        b_out = _attention(qmt, km, vmt, kmean)
        x = _mix_and_ffn(x, a_out, b_out, w_o[l], ln1_g[l], ln1_b[l], w_up[l], conv_w[l], conv_b[l],
                         w_down[l], ln2_g[l], ln2_b[l], alpha)
    return x
```

```python
import functools
import math

import jax
import jax.numpy as jnp
from jax import lax
from jax.experimental import pallas as pl
from jax.experimental.pallas import tpu as pltpu

MLA_HEADS = 8
MLA_NOPE = 64
MLA_ROPE = 32
MLA_V = 64
ROPE_THETA = 10000.0
MOBA_HEADS = 8
MOBA_HD = 64
MOBA_BLOCK = 256
MOBA_TOPK = 3
CONV_W = 3
EPS = 1e-5

HEAD_PAD = 128
HEADS_PER_GROUP = HEAD_PAD // MOBA_HD
BLOCKS_PER_TILE = 2
ATTN_TILE = BLOCKS_PER_TILE * MOBA_BLOCK
SLOPE_PIECES = 3
SUM_ROWS = 16
MASKED = -1e30
LOG2E = math.log2(math.e)
PROJ_ROWS = 512
FFN_ROWS = 512
FFN_PARTS = 2
FFN_CHUNK = 256
CARRY_ROWS = 8
VMEM_LIMIT = 48 * 1024 * 1024

F32 = jnp.float32
BF16 = jnp.bfloat16

_dot = functools.partial(jnp.dot, preferred_element_type=F32)


def _rms_norm(x, g):
    ms = jnp.mean(x * x, axis=-1, keepdims=True)
    return x * lax.rsqrt(ms + EPS) * g


def _layer_norm(x, g, b):
    mu = jnp.mean(x, axis=-1, keepdims=True)
    xc = x - mu
    var = jnp.mean(xc * xc, axis=-1, keepdims=True)
    return xc * lax.rsqrt(var + EPS) * g + b


def _dot_t(w_t, a):
    return lax.dot_general(w_t, a, (((1,), (1,)), ((), ())), preferred_element_type=F32)


def _proj_kernel(x_ref, win_ref, wqm_t_ref, wvm_t_ref, qg_ref, wqa_t_ref, kvg_ref, wk_ref, wv_t_ref,
                 cos_t_ref, sin_t_ref, tk1_ref, tk2_ref,
                 qat_ref, ka_ref, vat_ref, qmt_ref, km_ref, vmt_ref, kmean_ref):
    q_lora = wqa_t_ref.shape[1]
    kv_lora = wk_ref.shape[0]
    moba_w = km_ref.shape[2]
    rows = x_ref.shape[1]
    c0 = q_lora + kv_lora
    c1 = c0 + HEAD_PAD

    xb = x_ref[0].astype(BF16)
    c_q = _dot(xb, win_ref[:, 0:q_lora])
    kv_rope = _dot(xb, win_ref[:, q_lora:c1])
    c_kv = kv_rope[:, 0:kv_lora]
    rope_grp = kv_rope[:, kv_lora:kv_lora + HEAD_PAD]

    cqn = _rms_norm(c_q, qg_ref[...]).astype(BF16)
    ckn = _rms_norm(c_kv, kvg_ref[...]).astype(BF16)
    qa_t = _dot_t(wqa_t_ref[...], cqn)
    kk = _dot(ckn, wk_ref[...])
    vv_t = _dot_t(wv_t_ref[...], ckn)
    qm_t = _dot_t(wqm_t_ref[...], xb) * (MOBA_HD ** -0.5)
    km = _dot(xb, win_ref[:, c1:c1 + moba_w])
    vm_t = _dot_t(wvm_t_ref[...], xb)

    k_pe = rope_grp * tk1_ref[...] + pltpu.roll(rope_grp, HEAD_PAD - MLA_ROPE, 1) * tk2_ref[...]
    half = MLA_ROPE // 2
    cos_t, sin_t = cos_t_ref[...], sin_t_ref[...]
    scale = (MLA_NOPE + MLA_ROPE) ** -0.5 * LOG2E
    for h in range(MLA_HEADS):
        sl = slice(h * HEAD_PAD, (h + 1) * HEAD_PAD)
        base = h * HEAD_PAD
        x1 = qa_t[base + MLA_NOPE:base + MLA_NOPE + half, :]
        x2 = qa_t[base + MLA_NOPE + half:base + MLA_NOPE + MLA_ROPE, :]
        q_h = jnp.concatenate([qa_t[base:base + MLA_NOPE, :] * scale,
                               x1 * cos_t - x2 * sin_t, x2 * cos_t + x1 * sin_t,
                               qa_t[base + MLA_NOPE + MLA_ROPE:base + HEAD_PAD, :]], axis=0)
        qat_ref[0, sl, :] = q_h.astype(BF16)
        ka_ref[0, :, sl] = (kk[:, sl] + k_pe).astype(BF16)

    qmt_ref[0] = qm_t.astype(BF16)
    km_ref[0] = (km * LOG2E).astype(BF16)
    for j in range(rows // MOBA_BLOCK):
        blk = slice(j * MOBA_BLOCK, (j + 1) * MOBA_BLOCK)
        vat_ref[0, j] = vv_t[:, blk].astype(BF16)
        vmt_ref[0, j] = vm_t[:, blk].astype(BF16)
        kmean_ref[0, j] = jnp.mean(km[blk, :], axis=0, keepdims=True)


def _rope_tables(seq):
    half = MLA_ROPE // 2
    inv = ROPE_THETA ** (-jnp.arange(half, dtype=F32) / half)
    ang = jnp.arange(seq, dtype=F32)[:, None] * inv[None, :]
    cos, sin = jnp.cos(ang), jnp.sin(ang)
    z = lambda w: jnp.zeros((seq, w), F32)
    tail = HEAD_PAD - MLA_NOPE - MLA_ROPE
    tk1 = jnp.concatenate([z(MLA_NOPE), cos, cos, z(tail)], axis=1)
    tk2 = jnp.concatenate([z(MLA_NOPE), sin, sin, z(tail)], axis=1)
    scale = (MLA_NOPE + MLA_ROPE) ** -0.5 * LOG2E
    return (cos * scale).T, (sin * scale).T, tk1, tk2


def _rotate_half_cols(w):
    half = w.shape[-1] // 2
    return jnp.concatenate([-w[..., half:], w[..., :half]], axis=-1)


def _prep_proj_weights(w_in, w_uq, w_ukv):
    d_model = w_in.shape[0]
    q_lora = w_uq.shape[0]
    kv_lora = w_ukv.shape[0]
    moba_w = MOBA_HEADS * MOBA_HD
    o = q_lora + kv_lora
    k_rope = w_in[:, o:o + MLA_ROPE]
    rope_grp = jnp.concatenate(
        [jnp.zeros((d_model, MLA_NOPE), w_in.dtype), k_rope, _rotate_half_cols(k_rope)], axis=1)
    o += MLA_ROPE
    wqm, wkm, wvm = (w_in[:, o + j * moba_w:o + (j + 1) * moba_w] for j in range(3))
    win = jnp.concatenate([w_in[:, :q_lora + kv_lora], rope_grp, wkm], axis=1)

    wq = w_uq.reshape(q_lora, MLA_HEADS, MLA_NOPE + MLA_ROPE)
    pad = jnp.zeros((q_lora, MLA_HEADS, HEAD_PAD - MLA_NOPE - MLA_ROPE), w_uq.dtype)
    wqa = jnp.concatenate([wq, pad], axis=-1).reshape(q_lora, MLA_HEADS * HEAD_PAD)

    wkv = w_ukv.reshape(kv_lora, MLA_HEADS, MLA_NOPE + MLA_V)
    wk = jnp.concatenate([wkv[..., :MLA_NOPE],
                          jnp.zeros((kv_lora, MLA_HEADS, HEAD_PAD - MLA_NOPE), w_ukv.dtype)], axis=-1)
    wk = wk.reshape(kv_lora, MLA_HEADS * HEAD_PAD)
    wv = wkv[..., MLA_NOPE:].reshape(kv_lora, MLA_HEADS * MLA_V)
    return (win.astype(BF16), wqm.T.astype(BF16), wvm.T.astype(BF16),
            wqa.T.astype(BF16), wk.astype(BF16), wv.T.astype(BF16))


def _const_spec(shape):
    return pl.BlockSpec(shape, lambda *_: (0,) * len(shape))


def _input_projection(x, w_in, q_norm_g, w_uq, kv_norm_g, w_ukv):
    B, S, D = x.shape
    rows = PROJ_ROWS
    assert S % rows == 0 and rows % MOBA_BLOCK == 0
    nb = S // MOBA_BLOCK
    win, wqm_t, wvm_t, wqa_t, wk, wv_t = _prep_proj_weights(w_in, w_uq, w_ukv)
    cos_t, sin_t, tk1, tk2 = _rope_tables(S)
    mla_w = MLA_HEADS * HEAD_PAD
    v_w = MLA_HEADS * MLA_V
    moba_w = MOBA_HEADS * MOBA_HD
    blocks_per_step = rows // MOBA_BLOCK

    q_table_spec = pl.BlockSpec((cos_t.shape[0], rows), lambda b, i: (0, i))
    k_table_spec = pl.BlockSpec((rows, HEAD_PAD), lambda b, i: (i, 0))
    in_specs = [
        pl.BlockSpec((1, rows, D), lambda b, i: (b, i, 0)),
        _const_spec(win.shape), _const_spec(wqm_t.shape), _const_spec(wvm_t.shape),
        _const_spec((1, q_norm_g.shape[-1])), _const_spec(wqa_t.shape),
        _const_spec((1, kv_norm_g.shape[-1])), _const_spec(wk.shape), _const_spec(wv_t.shape),
        q_table_spec, q_table_spec, k_table_spec, k_table_spec,
    ]
    out_shape = [
        jax.ShapeDtypeStruct((B, mla_w, S), BF16),
        jax.ShapeDtypeStruct((B, S, mla_w), BF16),
        jax.ShapeDtypeStruct((B, nb, v_w, MOBA_BLOCK), BF16),
        jax.ShapeDtypeStruct((B, moba_w, S), BF16),
        jax.ShapeDtypeStruct((B, S, moba_w), BF16),
        jax.ShapeDtypeStruct((B, nb, moba_w, MOBA_BLOCK), BF16),
        jax.ShapeDtypeStruct((B, nb, 1, moba_w), F32),
    ]
    out_specs = [
        pl.BlockSpec((1, mla_w, rows), lambda b, i: (b, 0, i)),
        pl.BlockSpec((1, rows, mla_w), lambda b, i: (b, i, 0)),
        pl.BlockSpec((1, blocks_per_step, v_w, MOBA_BLOCK), lambda b, i: (b, i, 0, 0)),
        pl.BlockSpec((1, moba_w, rows), lambda b, i: (b, 0, i)),
        pl.BlockSpec((1, rows, moba_w), lambda b, i: (b, i, 0)),
        pl.BlockSpec((1, blocks_per_step, moba_w, MOBA_BLOCK), lambda b, i: (b, i, 0, 0)),
        pl.BlockSpec((1, blocks_per_step, 1, moba_w), lambda b, i: (b, i, 0, 0)),
    ]
    return pl.pallas_call(
        _proj_kernel,
        grid=(B, S // rows),
        in_specs=in_specs,
        out_specs=out_specs,
        out_shape=out_shape,
        compiler_params=pltpu.CompilerParams(
            dimension_semantics=("parallel", "parallel"), vmem_limit_bytes=VMEM_LIMIT),
        name="input_projection",
    )(x, win, wqm_t, wvm_t, q_norm_g.reshape(1, -1), wqa_t, kv_norm_g.reshape(1, -1), wk, wv_t,
      cos_t, sin_t, tk1, tk2)


def _moba_bias(kmean, qt, tile, slope):
    nb = kmean.shape[0]
    g = _dot(kmean, qt)
    blk = lax.broadcasted_iota(jnp.int32, g.shape, 0)
    blk_f = blk.astype(F32)
    own = tile * BLOCKS_PER_TILE + lax.broadcasted_iota(jnp.int32, g.shape, 1) // MOBA_BLOCK
    past = blk < own
    g = jnp.where(past, g, -jnp.inf)
    picked = blk < 0
    for _ in range(MOBA_TOPK):
        top = jnp.max(g, axis=0, keepdims=True)
        first = jnp.min(jnp.where(g == top, blk_f, float(nb)), axis=0, keepdims=True)
        hit = blk_f == first
        picked = picked | hit
        g = jnp.where(hit, -jnp.inf, g)
    keep = (picked & past) | (blk == own)
    tile_dist = ((tile - blk // BLOCKS_PER_TILE) * ATTN_TILE).astype(F32)
    return jnp.where(keep, 0.0, MASKED) - slope * tile_dist


def _slope_table(heads):
    slopes = LOG2E * 2.0 ** (-8.0 * jnp.arange(1, heads + 1, dtype=F32) / heads)
    cols, rest = [slopes], slopes
    for _ in range(SLOPE_PIECES):
        piece = rest.astype(BF16).astype(F32)
        cols.append(piece)
        rest = rest - piece
    return jnp.stack(cols, axis=1).reshape(-1)


def _attn_kernel(*refs, moba, heads):
    if moba:
        (slopes_ref, qt_ref, k_ref, vt_ref, kmean_ref, o_ref,
         m_ref, acc_ref, s_ref, smax_ref, qm_ref, bias_ref) = refs
    else:
        qt_ref, k_ref, vt_ref, o_ref, m_ref, acc_ref, s_ref, smax_ref = refs
    tile, blk = ATTN_TILE, MOBA_BLOCK
    assert BLOCKS_PER_TILE == 2
    own = pl.program_id(1)
    own_blk = own * BLOCKS_PER_TILE
    hd = vt_ref.shape[2] // heads
    qd = qt_ref.shape[1] // heads
    lo, hi = slice(0, blk), slice(blk, tile)

    causal = (lax.broadcasted_iota(jnp.int32, (blk, blk), 0)
              <= lax.broadcasted_iota(jnp.int32, (blk, blk), 1))

    if moba:
        r = lax.broadcasted_iota(jnp.int32, (tile, HEAD_PAD), 0)
        lane = lax.broadcasted_iota(jnp.int32, (tile, HEAD_PAD), 1)
        row_hi = (r // blk) * blk
        key_pos = jnp.where(lane < SLOPE_PIECES, row_hi,
                            jnp.where(lane < 2 * SLOPE_PIECES, r - row_hi, 0)).astype(F32).astype(BF16)
        r2 = lax.broadcasted_iota(jnp.int32, (HEAD_PAD, tile), 0)
        for h in range(heads):
            grp = h // HEADS_PER_GROUP
            rows = slice(grp * HEAD_PAD, (grp + 1) * HEAD_PAD)
            first = (h % HEADS_PER_GROUP) * qd
            qt = jnp.where((r2 >= first) & (r2 < first + qd), qt_ref[0, rows, :], jnp.zeros((), BF16))
            slope = slopes_ref[h * (SLOPE_PIECES + 1)]
            slope_rows = jnp.zeros((HEAD_PAD, tile), F32)
            for j in range(SLOPE_PIECES):
                piece = slopes_ref[h * (SLOPE_PIECES + 1) + 1 + j]
                slope_rows = jnp.where((r2 == j) | (r2 == SLOPE_PIECES + j), piece, slope_rows)
            qm_ref[h] = jnp.concatenate([qt, slope_rows.astype(BF16)], axis=0)
            bias = _moba_bias(kmean_ref[0, :, 0, rows].astype(BF16), qt, own, slope)
            for n in range(bias.shape[0]):
                bias_ref[h, n] = bias[n:n + 1, :]

    def keys(h, start, size, pos_rows):
        rows = pl.ds(pl.multiple_of(start, size), size)
        if moba:
            grp = h // HEADS_PER_GROUP
            return jnp.concatenate(
                [k_ref[0, rows, grp * HEAD_PAD:(grp + 1) * HEAD_PAD], key_pos[pos_rows, :]], axis=1)
        return k_ref[0, rows, h * HEAD_PAD:(h + 1) * HEAD_PAD]

    def queries(h):
        return qm_ref[h] if moba else qt_ref[0, h * HEAD_PAD:(h + 1) * HEAD_PAD, :]

    ones = jnp.ones((SUM_ROWS, tile), BF16)

    def values(h, first_block, blocks):
        vs = [vt_ref[0, first_block + j, h * hd:(h + 1) * hd, :] for j in range(blocks)]
        v = vs[0] if blocks == 1 else jnp.concatenate(vs, axis=1)
        return jnp.concatenate([v, ones[:, 0:blocks * blk]], axis=0)

    def issue_diag_scores(h):
        q = queries(h)
        s_ref[h, lo, :] = _dot(keys(h, own * tile, blk, lo), q)
        s_ref[h, hi, hi] = _dot(keys(h, own * tile + blk, blk, hi), q[:, hi])

    def diag_softmax_pv(h):
        s_ll, s_lh, s_hh = s_ref[h, lo, lo], s_ref[h, lo, hi], s_ref[h, hi, hi]
        if moba:
            b_lo, b_hi = bias_ref[h, own_blk], bias_ref[h, own_blk + 1]
            s_ll, s_lh, s_hh = s_ll + b_lo[:, lo], s_lh + b_lo[:, hi], s_hh + b_hi[:, hi]
        s_ll = jnp.where(causal, s_ll, MASKED)
        s_hh = jnp.where(causal, s_hh, MASKED)
        m_l = jnp.max(s_ll, axis=0, keepdims=True)
        m_h = jnp.maximum(jnp.max(s_lh, axis=0, keepdims=True), jnp.max(s_hh, axis=0, keepdims=True))
        p_l = jnp.exp2((s_ll - m_l).astype(BF16))
        p_h = jnp.concatenate([jnp.exp2((s_lh - m_h).astype(BF16)), jnp.exp2((s_hh - m_h).astype(BF16))], axis=0)
        m_ref[h] = jnp.concatenate([m_l, m_h], axis=1)
        acc_ref[h] = jnp.concatenate(
            [_dot(values(h, own_blk, 1), p_l), _dot(values(h, own_blk, BLOCKS_PER_TILE), p_h)], axis=1)

    def issue_scores(h, n):
        s = _dot(keys(h, n * tile, tile, slice(0, tile)), queries(h))
        s_ref[h] = s
        smax_ref[h, 0] = jnp.max(s[lo, :], axis=0, keepdims=True)
        smax_ref[h, 1] = jnp.max(s[hi, :], axis=0, keepdims=True)

    def softmax_pv(h, n):
        s = s_ref[h]
        m_old = m_ref[h]
        if moba:
            b_lo, b_hi = bias_ref[h, n * BLOCKS_PER_TILE], bias_ref[h, n * BLOCKS_PER_TILE + 1]
            m_new = jnp.maximum(m_old, jnp.maximum(smax_ref[h, 0] + b_lo, smax_ref[h, 1] + b_hi))
            p = jnp.concatenate([jnp.exp2((s[lo, :] - (m_new - b_lo)).astype(BF16)),
                                 jnp.exp2((s[hi, :] - (m_new - b_hi)).astype(BF16))], axis=0)
        else:
            m_new = jnp.maximum(m_old, jnp.maximum(smax_ref[h, 0], smax_ref[h, 1]))
            p = jnp.exp2((s - m_new).astype(BF16))
        m_ref[h] = m_new
        acc_ref[h] = jnp.exp2(m_old - m_new) * acc_ref[h] + _dot(values(h, n * BLOCKS_PER_TILE, BLOCKS_PER_TILE), p)

    for h in range(heads):
        issue_diag_scores(h)
    for h in range(heads):
        diag_softmax_pv(h)
        issue_scores(h, 0)

    def past_tile(n, carry):
        for h in range(heads):
            softmax_pv(h, n)
            issue_scores(h, n + 1)
        return carry

    lax.fori_loop(0, own - 1, past_tile, 0)

    @pl.when(own > 0)
    def _():
        for h in range(heads):
            softmax_pv(h, own - 1)

    out_t = jnp.concatenate(
        [acc_ref[h, 0:hd, :] / acc_ref[h, hd:hd + 1, :] for h in range(heads)], axis=0)
    o_ref[0] = out_t.astype(o_ref.dtype)


def _attention(qt, k, vt, kmean=None):
    moba = kmean is not None
    B, q_w, S = qt.shape
    nb, v_w = vt.shape[1], vt.shape[2]
    heads = MOBA_HEADS if moba else MLA_HEADS
    tile = ATTN_TILE
    assert S % tile == 0 and vt.shape[3] == MOBA_BLOCK
    k_w = k.shape[2]

    in_specs = [
        pl.BlockSpec((1, q_w, tile), lambda b, i, *_: (b, 0, i)),
        pl.BlockSpec((1, S, k_w), lambda b, i, *_: (b, 0, 0)),
        pl.BlockSpec((1, nb, v_w, MOBA_BLOCK), lambda b, i, *_: (b, 0, 0, 0)),
    ]
    args = [qt, k, vt]
    scratch = [
        pltpu.VMEM((heads, 1, tile), F32),
        pltpu.VMEM((heads, v_w // heads + SUM_ROWS, tile), F32),
        pltpu.VMEM((heads, tile, tile), F32),
        pltpu.VMEM((heads, BLOCKS_PER_TILE, 1, tile), F32),
    ]
    num_prefetch = 0
    if moba:
        in_specs.append(pl.BlockSpec((1, nb, 1, k_w), lambda b, i, *_: (b, 0, 0, 0)))
        args = [_slope_table(heads)] + args + [kmean]
        scratch += [
            pltpu.VMEM((heads, 2 * HEAD_PAD, tile), BF16),
            pltpu.VMEM((heads, nb, 1, tile), F32),
        ]
        num_prefetch = 1
    grid_spec = pltpu.PrefetchScalarGridSpec(
        num_scalar_prefetch=num_prefetch,
        grid=(B, S // tile),
        in_specs=in_specs,
        out_specs=pl.BlockSpec((1, v_w, tile), lambda b, i, *_: (b, 0, i)),
        scratch_shapes=scratch,
    )
    return pl.pallas_call(
        functools.partial(_attn_kernel, moba=moba, heads=heads),
        grid_spec=grid_spec,
        out_shape=jax.ShapeDtypeStruct((B, v_w, S), BF16),
        compiler_params=pltpu.CompilerParams(
            dimension_semantics=("parallel", "arbitrary"), vmem_limit_bytes=VMEM_LIMIT),
        name="moba_attention" if moba else "mla_attention",
    )(*args)


def _ffn_kernel(x_ref, a_ref, b_ref, woa_ref, wob_ref, g1_ref, b1_ref,
                wup_ref, cw_ref, cb_ref, wdn_ref, g2_ref, b2_ref,
                o_ref, act_ref, hbuf_ref, carry_ref, *, alpha):
    rows = x_ref.shape[1]
    d_ff = wdn_ref.shape[0]
    chunk = FFN_CHUNK

    @pl.when(pl.program_id(1) == 0)
    def _():
        carry_ref[...] = jnp.zeros(carry_ref.shape, F32)

    part_rows = rows // FFN_PARTS
    parts = [slice(p * part_rows, (p + 1) * part_rows) for p in range(FFN_PARTS)]

    dot_at = lambda a_t, w: lax.dot_general(a_t, w, (((0,), (0,)), ((), ())), preferred_element_type=F32)
    mixes = [dot_at(a_ref[0, :, r], woa_ref[...]) + dot_at(b_ref[0, :, r], wob_ref[...]) for r in parts]
    x1 = [_layer_norm(alpha * x_ref[0, r, :] + mix, g1_ref[...], b1_ref[...]) for r, mix in zip(parts, mixes)]
    x1b = [v.astype(BF16) for v in x1]

    def conv_branch(p, col, branch, prev):
        cols = slice(col, col + chunk)
        slot = p * 2 + branch
        h = _dot(x1b[p], wup_ref[:, cols])
        hbuf_ref[slot, 0:CARRY_ROWS, :] = prev
        hbuf_ref[slot, CARRY_ROWS:CARRY_ROWS + part_rows, :] = h
        cw = cw_ref[:, cols]
        y = h * cw[CONV_W - 1:CONV_W, :] + cb_ref[:, cols]
        for tap in range(CONV_W - 1):
            back = CONV_W - 1 - tap
            shifted = hbuf_ref[slot, CARRY_ROWS - back:CARRY_ROWS - back + part_rows, :]
            y = y + shifted * cw[tap:tap + 1, :]
        return y, h[part_rows - CARRY_ROWS:, :]

    for c in range(d_ff // chunk):
        g_col, u_col = c * chunk, d_ff + c * chunk
        g_prev = carry_ref[:, g_col:g_col + chunk]
        u_prev = carry_ref[:, u_col:u_col + chunk]
        for p, r in enumerate(parts):
            gate, g_prev = conv_branch(p, g_col, 0, g_prev)
            up, u_prev = conv_branch(p, u_col, 1, u_prev)
            act = gate * (1.0 / (1.0 + jnp.exp(-gate))) * up
            act_ref[r, c * chunk:(c + 1) * chunk] = act.astype(BF16)
        carry_ref[:, g_col:g_col + chunk] = g_prev
        carry_ref[:, u_col:u_col + chunk] = u_prev

    ys = [_dot(act_ref[r, :], wdn_ref[...]) for r in parts]
    for r, v, y in zip(parts, x1, ys):
        o_ref[0, r, :] = _layer_norm(alpha * v + y, g2_ref[...], b2_ref[...])


def _mix_and_ffn(x, a_out, b_out, w_o, ln1_g, ln1_b, w_up, conv_w, conv_b, w_down, ln2_g, ln2_b, alpha):
    B, S, D = x.shape
    rows = FFN_ROWS
    d_ff = w_down.shape[0]
    assert S % rows == 0 and d_ff % FFN_CHUNK == 0 and conv_w.shape[0] == CONV_W
    a_w = a_out.shape[1]
    row_spec = lambda w: pl.BlockSpec((1, rows, w), lambda b, i: (b, i, 0))
    col_spec = lambda w: pl.BlockSpec((1, w, rows), lambda b, i: (b, 0, i))
    vec = lambda v: v.reshape(1, -1).astype(F32)
    in_specs = [
        row_spec(D), col_spec(a_w), col_spec(b_out.shape[1]),
        _const_spec((a_w, D)), _const_spec((w_o.shape[0] - a_w, D)),
        _const_spec((1, D)), _const_spec((1, D)),
        _const_spec(w_up.shape), _const_spec(conv_w.shape), _const_spec((1, 2 * d_ff)),
        _const_spec(w_down.shape), _const_spec((1, D)), _const_spec((1, D)),
    ]
    return pl.pallas_call(
        functools.partial(_ffn_kernel, alpha=alpha),
        grid=(B, S // rows),
        in_specs=in_specs,
        out_specs=row_spec(D),
        out_shape=jax.ShapeDtypeStruct((B, S, D), x.dtype),
        scratch_shapes=[
            pltpu.VMEM((rows, d_ff), BF16),
            pltpu.VMEM((2 * FFN_PARTS, CARRY_ROWS + rows // FFN_PARTS, FFN_CHUNK), F32),
            pltpu.VMEM((CARRY_ROWS, 2 * d_ff), F32),
        ],
        compiler_params=pltpu.CompilerParams(
            dimension_semantics=("parallel", "arbitrary"), vmem_limit_bytes=VMEM_LIMIT),
        name="mix_ffn",
    )(x, a_out, b_out, w_o[:a_w].astype(BF16), w_o[a_w:].astype(BF16), vec(ln1_g), vec(ln1_b),
      w_up.astype(BF16), conv_w.astype(F32), vec(conv_b), w_down.astype(BF16), vec(ln2_g), vec(ln2_b))


def kernel(x, w_in, q_norm_g, w_uq, kv_norm_g, w_ukv, w_o, ln1_g, ln1_b, w_up, conv_w, conv_b, w_down, ln2_g, ln2_b):
    depth = w_in.shape[0]
    alpha = (2.0 * depth) ** 0.25
    for l in range(depth):
        qat, ka, vat, qmt, km, vmt, kmean = _input_projection(
            x, w_in[l], q_norm_g[l], w_uq[l], kv_norm_g[l], w_ukv[l])
        a_out = _attention(qat, ka, vat)
        b_out = _attention(qmt, km, vmt, kmean)
        x = _mix_and_ffn(x, a_out, b_out, w_o[l], ln1_g[l], ln1_b[l], w_up[l], conv_w[l], conv_b[l],
                         w_down[l], ln2_g[l], ln2_b[l], alpha)
    return x
```

```python
import functools
import math

import jax
import jax.numpy as jnp
from jax import lax
from jax.experimental import pallas as pl
from jax.experimental.pallas import tpu as pltpu

MLA_HEADS = 8
MLA_NOPE = 64
MLA_ROPE = 32
MLA_V = 64
ROPE_THETA = 10000.0
MOBA_HEADS = 8
MOBA_HD = 64
MOBA_BLOCK = 256
MOBA_TOPK = 3
CONV_W = 3
EPS = 1e-5

HEAD_PAD = 128
HEADS_PER_GROUP = HEAD_PAD // MOBA_HD
BLOCKS_PER_TILE = 2
ATTN_TILE = BLOCKS_PER_TILE * MOBA_BLOCK
SLOPE_PIECES = 3
SUM_ROWS = 16
MASKED = -1e30
LOG2E = math.log2(math.e)
PROJ_ROWS = 512
FFN_ROWS = 512
FFN_PARTS = 2
FFN_CHUNK = 256
CARRY_ROWS = 8
VMEM_LIMIT = 48 * 1024 * 1024

F32 = jnp.float32
BF16 = jnp.bfloat16

_dot = functools.partial(jnp.dot, preferred_element_type=F32)


def _rms_norm(x, g):
    ms = jnp.mean(x * x, axis=-1, keepdims=True)
    return x * lax.rsqrt(ms + EPS) * g


def _layer_norm(x, g, b):
    mu = jnp.mean(x, axis=-1, keepdims=True)
    xc = x - mu
    var = jnp.mean(xc * xc, axis=-1, keepdims=True)
    return xc * lax.rsqrt(var + EPS) * g + b


def _dot_t(w_t, a):
    return lax.dot_general(w_t, a, (((1,), (1,)), ((), ())), preferred_element_type=F32)


def _proj_kernel(x_ref, win_ref, wqm_t_ref, wvm_t_ref, qg_ref, wqa_t_ref, kvg_ref, wk_ref, wv_t_ref,
                 cos_t_ref, sin_t_ref, tk1_ref, tk2_ref,
                 qat_ref, ka_ref, vat_ref, qmt_ref, km_ref, vmt_ref, kmean_ref):
    q_lora = wqa_t_ref.shape[1]
    kv_lora = wk_ref.shape[0]
    moba_w = km_ref.shape[2]
    rows = x_ref.shape[1]
    c0 = q_lora + kv_lora
    c1 = c0 + HEAD_PAD

    xb = x_ref[0].astype(BF16)
    c_q = _dot(xb, win_ref[:, 0:q_lora])
    kv_rope = _dot(xb, win_ref[:, q_lora:c1])
    c_kv = kv_rope[:, 0:kv_lora]
    rope_grp = kv_rope[:, kv_lora:kv_lora + HEAD_PAD]

    cqn = _rms_norm(c_q, qg_ref[...]).astype(BF16)
    ckn = _rms_norm(c_kv, kvg_ref[...]).astype(BF16)
    qa_t = _dot_t(wqa_t_ref[...], cqn)
    kk = _dot(ckn, wk_ref[...])
    vv_t = _dot_t(wv_t_ref[...], ckn)
    qm_t = _dot_t(wqm_t_ref[...], xb) * (MOBA_HD ** -0.5)
    km = _dot(xb, win_ref[:, c1:c1 + moba_w])
    vm_t = _dot_t(wvm_t_ref[...], xb)

    k_pe = rope_grp * tk1_ref[...] + pltpu.roll(rope_grp, HEAD_PAD - MLA_ROPE, 1) * tk2_ref[...]
    half = MLA_ROPE // 2
    cos_t, sin_t = cos_t_ref[...], sin_t_ref[...]
    scale = (MLA_NOPE + MLA_ROPE) ** -0.5 * LOG2E
    for h in range(MLA_HEADS):
        sl = slice(h * HEAD_PAD, (h + 1) * HEAD_PAD)
        base = h * HEAD_PAD
        x1 = qa_t[base + MLA_NOPE:base + MLA_NOPE + half, :]
        x2 = qa_t[base + MLA_NOPE + half:base + MLA_NOPE + MLA_ROPE, :]
        q_h = jnp.concatenate([qa_t[base:base + MLA_NOPE, :] * scale,
                               x1 * cos_t - x2 * sin_t, x2 * cos_t + x1 * sin_t,
                               qa_t[base + MLA_NOPE + MLA_ROPE:base + HEAD_PAD, :]], axis=0)
        qat_ref[0, sl, :] = q_h.astype(BF16)
        ka_ref[0, :, sl] = (kk[:, sl] + k_pe).astype(BF16)

    qmt_ref[0] = qm_t.astype(BF16)
    km_ref[0] = (km * LOG2E).astype(BF16)
    for j in range(rows // MOBA_BLOCK):
        blk = slice(j * MOBA_BLOCK, (j + 1) * MOBA_BLOCK)
        vat_ref[0, j] = vv_t[:, blk].astype(BF16)
        vmt_ref[0, j] = vm_t[:, blk].astype(BF16)
        kmean_ref[0, j] = jnp.mean(km[blk, :], axis=0, keepdims=True)


def _rope_tables(seq):
    half = MLA_ROPE // 2
    inv = ROPE_THETA ** (-jnp.arange(half, dtype=F32) / half)
    ang = jnp.arange(seq, dtype=F32)[:, None] * inv[None, :]
    cos, sin = jnp.cos(ang), jnp.sin(ang)
    z = lambda w: jnp.zeros((seq, w), F32)
    tail = HEAD_PAD - MLA_NOPE - MLA_ROPE
    tk1 = jnp.concatenate([z(MLA_NOPE), cos, cos, z(tail)], axis=1)
    tk2 = jnp.concatenate([z(MLA_NOPE), sin, sin, z(tail)], axis=1)
    scale = (MLA_NOPE + MLA_ROPE) ** -0.5 * LOG2E
    return (cos * scale).T, (sin * scale).T, tk1, tk2


def _rotate_half_cols(w):
    half = w.shape[-1] // 2
    return jnp.concatenate([-w[..., half:], w[..., :half]], axis=-1)


def _prep_proj_weights(w_in, w_uq, w_ukv):
    d_model = w_in.shape[0]
    q_lora = w_uq.shape[0]
    kv_lora = w_ukv.shape[0]
    moba_w = MOBA_HEADS * MOBA_HD
    o = q_lora + kv_lora
    k_rope = w_in[:, o:o + MLA_ROPE]
    rope_grp = jnp.concatenate(
        [jnp.zeros((d_model, MLA_NOPE), w_in.dtype), k_rope, _rotate_half_cols(k_rope)], axis=1)
    o += MLA_ROPE
    wqm, wkm, wvm = (w_in[:, o + j * moba_w:o + (j + 1) * moba_w] for j in range(3))
    win = jnp.concatenate([w_in[:, :q_lora + kv_lora], rope_grp, wkm], axis=1)

    wq = w_uq.reshape(q_lora, MLA_HEADS, MLA_NOPE + MLA_ROPE)
    pad = jnp.zeros((q_lora, MLA_HEADS, HEAD_PAD - MLA_NOPE - MLA_ROPE), w_uq.dtype)
    wqa = jnp.concatenate([wq, pad], axis=-1).reshape(q_lora, MLA_HEADS * HEAD_PAD)

    wkv = w_ukv.reshape(kv_lora, MLA_HEADS, MLA_NOPE + MLA_V)
    wk = jnp.concatenate([wkv[..., :MLA_NOPE],
                          jnp.zeros((kv_lora, MLA_HEADS, HEAD_PAD - MLA_NOPE), w_ukv.dtype)], axis=-1)
    wk = wk.reshape(kv_lora, MLA_HEADS * HEAD_PAD)
    wv = wkv[..., MLA_NOPE:].reshape(kv_lora, MLA_HEADS * MLA_V)
    return (win.astype(BF16), wqm.T.astype(BF16), wvm.T.astype(BF16),
            wqa.T.astype(BF16), wk.astype(BF16), wv.T.astype(BF16))


def _const_spec(shape):
    return pl.BlockSpec(shape, lambda *_: (0,) * len(shape))


def _input_projection(x, w_in, q_norm_g, w_uq, kv_norm_g, w_ukv):
    B, S, D = x.shape
    rows = PROJ_ROWS
    assert S % rows == 0 and rows % MOBA_BLOCK == 0
    nb = S // MOBA_BLOCK
    win, wqm_t, wvm_t, wqa_t, wk, wv_t = _prep_proj_weights(w_in, w_uq, w_ukv)
    cos_t, sin_t, tk1, tk2 = _rope_tables(S)
    mla_w = MLA_HEADS * HEAD_PAD
    v_w = MLA_HEADS * MLA_V
    moba_w = MOBA_HEADS * MOBA_HD
    blocks_per_step = rows // MOBA_BLOCK

    q_table_spec = pl.BlockSpec((cos_t.shape[0], rows), lambda b, i: (0, i))
    k_table_spec = pl.BlockSpec((rows, HEAD_PAD), lambda b, i: (i, 0))
    in_specs = [
        pl.BlockSpec((1, rows, D), lambda b, i: (b, i, 0)),
        _const_spec(win.shape), _const_spec(wqm_t.shape), _const_spec(wvm_t.shape),
        _const_spec((1, q_norm_g.shape[-1])), _const_spec(wqa_t.shape),
        _const_spec((1, kv_norm_g.shape[-1])), _const_spec(wk.shape), _const_spec(wv_t.shape),
        q_table_spec, q_table_spec, k_table_spec, k_table_spec,
    ]
    out_shape = [
        jax.ShapeDtypeStruct((B, mla_w, S), BF16),
        jax.ShapeDtypeStruct((B, S, mla_w), BF16),
        jax.ShapeDtypeStruct((B, nb, v_w, MOBA_BLOCK), BF16),
        jax.ShapeDtypeStruct((B, moba_w, S), BF16),
        jax.ShapeDtypeStruct((B, S, moba_w), BF16),
        jax.ShapeDtypeStruct((B, nb, moba_w, MOBA_BLOCK), BF16),
        jax.ShapeDtypeStruct((B, nb, 1, moba_w), F32),
    ]
    out_specs = [
        pl.BlockSpec((1, mla_w, rows), lambda b, i: (b, 0, i)),
        pl.BlockSpec((1, rows, mla_w), lambda b, i: (b, i, 0)),
        pl.BlockSpec((1, blocks_per_step, v_w, MOBA_BLOCK), lambda b, i: (b, i, 0, 0)),
        pl.BlockSpec((1, moba_w, rows), lambda b, i: (b, 0, i)),
        pl.BlockSpec((1, rows, moba_w), lambda b, i: (b, i, 0)),
        pl.BlockSpec((1, blocks_per_step, moba_w, MOBA_BLOCK), lambda b, i: (b, i, 0, 0)),
        pl.BlockSpec((1, blocks_per_step, 1, moba_w), lambda b, i: (b, i, 0, 0)),
    ]
    return pl.pallas_call(
        _proj_kernel,
        grid=(B, S // rows),
        in_specs=in_specs,
        out_specs=out_specs,
        out_shape=out_shape,
        compiler_params=pltpu.CompilerParams(
            dimension_semantics=("parallel", "parallel"), vmem_limit_bytes=VMEM_LIMIT),
        name="input_projection",
    )(x, win, wqm_t, wvm_t, q_norm_g.reshape(1, -1), wqa_t, kv_norm_g.reshape(1, -1), wk, wv_t,
      cos_t, sin_t, tk1, tk2)


def _moba_bias(kmean, qt, tile, slope):
    nb = kmean.shape[0]
    g = _dot(kmean, qt)
    blk = lax.broadcasted_iota(jnp.int32, g.shape, 0)
    blk_f = blk.astype(F32)
    own = tile * BLOCKS_PER_TILE + lax.broadcasted_iota(jnp.int32, g.shape, 1) // MOBA_BLOCK
    past = blk < own
    g = jnp.where(past, g, -jnp.inf)
    picked = blk < 0
    for _ in range(MOBA_TOPK):
        top = jnp.max(g, axis=0, keepdims=True)
        first = jnp.min(jnp.where(g == top, blk_f, float(nb)), axis=0, keepdims=True)
        hit = blk_f == first
        picked = picked | hit
        g = jnp.where(hit, -jnp.inf, g)
    keep = (picked & past) | (blk == own)
    tile_dist = ((tile - blk // BLOCKS_PER_TILE) * ATTN_TILE).astype(F32)
    return jnp.where(keep, 0.0, MASKED) - slope * tile_dist


def _slope_table(heads):
    slopes = LOG2E * 2.0 ** (-8.0 * jnp.arange(1, heads + 1, dtype=F32) / heads)
    cols, rest = [slopes], slopes
    for _ in range(SLOPE_PIECES):
        piece = rest.astype(BF16).astype(F32)
        cols.append(piece)
        rest = rest - piece
    return jnp.stack(cols, axis=1).reshape(-1)


def _attn_kernel(*refs, moba, heads):
    if moba:
        (slopes_ref, qt_ref, k_ref, vt_ref, kmean_ref, o_ref,
         m_ref, acc_ref, s_ref, smax_ref, qm_ref, bias_ref) = refs
    else:
        qt_ref, k_ref, vt_ref, o_ref, m_ref, acc_ref, s_ref, smax_ref = refs
    tile, blk = ATTN_TILE, MOBA_BLOCK
    assert BLOCKS_PER_TILE == 2
    own = pl.program_id(1)
    own_blk = own * BLOCKS_PER_TILE
    hd = vt_ref.shape[2] // heads
    qd = qt_ref.shape[1] // heads
    lo, hi = slice(0, blk), slice(blk, tile)

    causal = (lax.broadcasted_iota(jnp.int32, (blk, blk), 0)
              <= lax.broadcasted_iota(jnp.int32, (blk, blk), 1))

    if moba:
        r = lax.broadcasted_iota(jnp.int32, (tile, HEAD_PAD), 0)
        lane = lax.broadcasted_iota(jnp.int32, (tile, HEAD_PAD), 1)
        row_hi = (r // blk) * blk
        key_pos = jnp.where(lane < SLOPE_PIECES, row_hi,
                            jnp.where(lane < 2 * SLOPE_PIECES, r - row_hi, 0)).astype(F32).astype(BF16)
        r2 = lax.broadcasted_iota(jnp.int32, (HEAD_PAD, tile), 0)
        for h in range(heads):
            grp = h // HEADS_PER_GROUP
            rows = slice(grp * HEAD_PAD, (grp + 1) * HEAD_PAD)
            first = (h % HEADS_PER_GROUP) * qd
            qt = jnp.where((r2 >= first) & (r2 < first + qd), qt_ref[0, rows, :], jnp.zeros((), BF16))
            slope = slopes_ref[h * (SLOPE_PIECES + 1)]
            slope_rows = jnp.zeros((HEAD_PAD, tile), F32)
            for j in range(SLOPE_PIECES):
                piece = slopes_ref[h * (SLOPE_PIECES + 1) + 1 + j]
                slope_rows = jnp.where((r2 == j) | (r2 == SLOPE_PIECES + j), piece, slope_rows)
            qm_ref[h] = jnp.concatenate([qt, slope_rows.astype(BF16)], axis=0)
            bias = _moba_bias(kmean_ref[0, :, 0, rows].astype(BF16), qt, own, slope)
            for n in range(bias.shape[0]):
                bias_ref[h, n] = bias[n:n + 1, :]

    def keys(h, start, size, pos_rows):
        rows = pl.ds(pl.multiple_of(start, size), size)
        if moba:
            grp = h // HEADS_PER_GROUP
            return jnp.concatenate(
                [k_ref[0, rows, grp * HEAD_PAD:(grp + 1) * HEAD_PAD], key_pos[pos_rows, :]], axis=1)
        return k_ref[0, rows, h * HEAD_PAD:(h + 1) * HEAD_PAD]

    def queries(h):
        return qm_ref[h] if moba else qt_ref[0, h * HEAD_PAD:(h + 1) * HEAD_PAD, :]

    ones = jnp.ones((SUM_ROWS, tile), BF16)

    def values(h, first_block, blocks):
        vs = [vt_ref[0, first_block + j, h * hd:(h + 1) * hd, :] for j in range(blocks)]
        v = vs[0] if blocks == 1 else jnp.concatenate(vs, axis=1)
        return jnp.concatenate([v, ones[:, 0:blocks * blk]], axis=0)

    def issue_diag_scores(h):
        q = queries(h)
        s_ref[h, lo, :] = _dot(keys(h, own * tile, blk, lo), q)
        s_ref[h, hi, hi] = _dot(keys(h, own * tile + blk, blk, hi), q[:, hi])

    def diag_softmax_pv(h):
        s_ll, s_lh, s_hh = s_ref[h, lo, lo], s_ref[h, lo, hi], s_ref[h, hi, hi]
        if moba:
            b_lo, b_hi = bias_ref[h, own_blk], bias_ref[h, own_blk + 1]
            s_ll, s_lh, s_hh = s_ll + b_lo[:, lo], s_lh + b_lo[:, hi], s_hh + b_hi[:, hi]
        s_ll = jnp.where(causal, s_ll, MASKED)
        s_hh = jnp.where(causal, s_hh, MASKED)
        m_l = jnp.max(s_ll, axis=0, keepdims=True)
        m_h = jnp.maximum(jnp.max(s_lh, axis=0, keepdims=True), jnp.max(s_hh, axis=0, keepdims=True))
        p_l = jnp.exp2((s_ll - m_l).astype(BF16))
        p_h = jnp.concatenate([jnp.exp2((s_lh - m_h).astype(BF16)), jnp.exp2((s_hh - m_h).astype(BF16))], axis=0)
        m_ref[h] = jnp.concatenate([m_l, m_h], axis=1)
        acc_ref[h] = jnp.concatenate(
            [_dot(values(h, own_blk, 1), p_l), _dot(values(h, own_blk, BLOCKS_PER_TILE), p_h)], axis=1)

    def issue_scores(h, n):
        s = _dot(keys(h, n * tile, tile, slice(0, tile)), queries(h))
        s_ref[h] = s
        smax_ref[h, 0] = jnp.max(s[lo, :], axis=0, keepdims=True)
        smax_ref[h, 1] = jnp.max(s[hi, :], axis=0, keepdims=True)

    def softmax_pv(h, n):
        s = s_ref[h]
        m_old = m_ref[h]
        if moba:
            b_lo, b_hi = bias_ref[h, n * BLOCKS_PER_TILE], bias_ref[h, n * BLOCKS_PER_TILE + 1]
            m_new = jnp.maximum(m_old, jnp.maximum(smax_ref[h, 0] + b_lo, smax_ref[h, 1] + b_hi))
            p = jnp.concatenate([jnp.exp2((s[lo, :] - (m_new - b_lo)).astype(BF16)),
                                 jnp.exp2((s[hi, :] - (m_new - b_hi)).astype(BF16))], axis=0)
        else:
            m_new = jnp.maximum(m_old, jnp.maximum(smax_ref[h, 0], smax_ref[h, 1]))
            p = jnp.exp2((s - m_new).astype(BF16))
        m_ref[h] = m_new
        acc_ref[h] = jnp.exp2(m_old - m_new) * acc_ref[h] + _dot(values(h, n * BLOCKS_PER_TILE, BLOCKS_PER_TILE), p)

    for h in range(heads):
        issue_diag_scores(h)
    for h in range(heads):
        diag_softmax_pv(h)
        issue_scores(h, 0)

    def past_tile(n, carry):
        for h in range(heads):
            softmax_pv(h, n)
            issue_scores(h, n + 1)
        return carry

    lax.fori_loop(0, own - 1, past_tile, 0)

    @pl.when(own > 0)
    def _():
        for h in range(heads):
            softmax_pv(h, own - 1)

    out_t = jnp.concatenate(
        [acc_ref[h, 0:hd, :] / acc_ref[h, hd:hd + 1, :] for h in range(heads)], axis=0)
    o_ref[0] = out_t.astype(o_ref.dtype)


def _attention(qt, k, vt, kmean=None):
    moba = kmean is not None
    B, q_w, S = qt.shape
    nb, v_w = vt.shape[1], vt.shape[2]
    heads = MOBA_HEADS if moba else MLA_HEADS
    tile = ATTN_TILE
    assert S % tile == 0 and vt.shape[3] == MOBA_BLOCK
    k_w = k.shape[2]

    in_specs = [
        pl.BlockSpec((1, q_w, tile), lambda b, i, *_: (b, 0, i)),
        pl.BlockSpec((1, S, k_w), lambda b, i, *_: (b, 0, 0)),
        pl.BlockSpec((1, nb, v_w, MOBA_BLOCK), lambda b, i, *_: (b, 0, 0, 0)),
    ]
    args = [qt, k, vt]
    scratch = [
        pltpu.VMEM((heads, 1, tile), F32),
        pltpu.VMEM((heads, v_w // heads + SUM_ROWS, tile), F32),
        pltpu.VMEM((heads, tile, tile), F32),
        pltpu.VMEM((heads, BLOCKS_PER_TILE, 1, tile), F32),
    ]
    num_prefetch = 0
    if moba:
        in_specs.append(pl.BlockSpec((1, nb, 1, k_w), lambda b, i, *_: (b, 0, 0, 0)))
        args = [_slope_table(heads)] + args + [kmean]
        scratch += [
            pltpu.VMEM((heads, 2 * HEAD_PAD, tile), BF16),
            pltpu.VMEM((heads, nb, 1, tile), F32),
        ]
        num_prefetch = 1
    grid_spec = pltpu.PrefetchScalarGridSpec(
        num_scalar_prefetch=num_prefetch,
        grid=(B, S // tile),
        in_specs=in_specs,
        out_specs=pl.BlockSpec((1, v_w, tile), lambda b, i, *_: (b, 0, i)),
        scratch_shapes=scratch,
    )
    return pl.pallas_call(
        functools.partial(_attn_kernel, moba=moba, heads=heads),
        grid_spec=grid_spec,
        out_shape=jax.ShapeDtypeStruct((B, v_w, S), BF16),
        compiler_params=pltpu.CompilerParams(
            dimension_semantics=("parallel", "arbitrary"), vmem_limit_bytes=VMEM_LIMIT),
        name="moba_attention" if moba else "mla_attention",
    )(*args)


def _ffn_kernel(x_ref, a_ref, b_ref, woa_ref, wob_ref, g1_ref, b1_ref,
                wup_ref, cw_ref, cb_ref, wdn_ref, g2_ref, b2_ref,
                o_ref, act_ref, carry_ref, *, alpha):
    rows = x_ref.shape[1]
    d_ff = wdn_ref.shape[0]
    chunk = FFN_CHUNK

    @pl.when(pl.program_id(1) == 0)
    def _():
        carry_ref[...] = jnp.zeros(carry_ref.shape, F32)

    part_rows = rows // FFN_PARTS
    parts = [slice(p * part_rows, (p + 1) * part_rows) for p in range(FFN_PARTS)]

    dot_at = lambda a_t, w: lax.dot_general(a_t, w, (((0,), (0,)), ((), ())), preferred_element_type=F32)
    mixes = [dot_at(a_ref[0, :, r], woa_ref[...]) + dot_at(b_ref[0, :, r], wob_ref[...]) for r in parts]
    x1 = [_layer_norm(alpha * x_ref[0, r, :] + mix, g1_ref[...], b1_ref[...]) for r, mix in zip(parts, mixes)]
    x1b = [v.astype(BF16) for v in x1]

    def conv_branch(p, col, prev):
        cols = slice(col, col + chunk)
        h = _dot(x1b[p], wup_ref[:, cols])
        cw = cw_ref[:, cols]
        y = h * cw[CONV_W - 1:CONV_W, :] + cb_ref[:, cols]
        top = lax.broadcasted_iota(jnp.int32, (CARRY_ROWS, chunk), 0)
        for tap in range(CONV_W - 1):
            back = CONV_W - 1 - tap
            moved = pltpu.roll(h, back, 0)
            head = jnp.where(top < back, pltpu.roll(prev, back, 0), moved[0:CARRY_ROWS, :])
            shifted = jnp.concatenate([head, moved[CARRY_ROWS:, :]], axis=0)
            y = y + shifted * cw[tap:tap + 1, :]
        return y, h[part_rows - CARRY_ROWS:, :]

    for c in range(d_ff // chunk):
        g_col, u_col = c * chunk, d_ff + c * chunk
        g_prev = carry_ref[:, g_col:g_col + chunk]
        u_prev = carry_ref[:, u_col:u_col + chunk]
        for p, r in enumerate(parts):
            gate, g_prev = conv_branch(p, g_col, g_prev)
            up, u_prev = conv_branch(p, u_col, u_prev)
            act = gate * (1.0 / (1.0 + jnp.exp(-gate))) * up
            act_ref[r, c * chunk:(c + 1) * chunk] = act.astype(BF16)
        carry_ref[:, g_col:g_col + chunk] = g_prev
        carry_ref[:, u_col:u_col + chunk] = u_prev

    ys = [_dot(act_ref[r, :], wdn_ref[...]) for r in parts]
    for r, v, y in zip(parts, x1, ys):
        o_ref[0, r, :] = _layer_norm(alpha * v + y, g2_ref[...], b2_ref[...])


def _mix_and_ffn(x, a_out, b_out, w_o, ln1_g, ln1_b, w_up, conv_w, conv_b, w_down, ln2_g, ln2_b, alpha):
    B, S, D = x.shape
    rows = FFN_ROWS
    d_ff = w_down.shape[0]
    assert S % rows == 0 and d_ff % FFN_CHUNK == 0 and conv_w.shape[0] == CONV_W
    a_w = a_out.shape[1]
    row_spec = lambda w: pl.BlockSpec((1, rows, w), lambda b, i: (b, i, 0))
    col_spec = lambda w: pl.BlockSpec((1, w, rows), lambda b, i: (b, 0, i))
    vec = lambda v: v.reshape(1, -1).astype(F32)
    in_specs = [
        row_spec(D), col_spec(a_w), col_spec(b_out.shape[1]),
        _const_spec((a_w, D)), _const_spec((w_o.shape[0] - a_w, D)),
        _const_spec((1, D)), _const_spec((1, D)),
        _const_spec(w_up.shape), _const_spec(conv_w.shape), _const_spec((1, 2 * d_ff)),
        _const_spec(w_down.shape), _const_spec((1, D)), _const_spec((1, D)),
    ]
    return pl.pallas_call(
        functools.partial(_ffn_kernel, alpha=alpha),
        grid=(B, S // rows),
        in_specs=in_specs,
        out_specs=row_spec(D),
        out_shape=jax.ShapeDtypeStruct((B, S, D), x.dtype),
        scratch_shapes=[
            pltpu.VMEM((rows, d_ff), BF16),
            pltpu.VMEM((CARRY_ROWS, 2 * d_ff), F32),
        ],
        compiler_params=pltpu.CompilerParams(
            dimension_semantics=("parallel", "arbitrary"), vmem_limit_bytes=VMEM_LIMIT),
        name="mix_ffn",
    )(x, a_out, b_out, w_o[:a_w].astype(BF16), w_o[a_w:].astype(BF16), vec(ln1_g), vec(ln1_b),
      w_up.astype(BF16), conv_w.astype(F32), vec(conv_b), w_down.astype(BF16), vec(ln2_g), vec(ln2_b))


def kernel(x, w_in, q_norm_g, w_uq, kv_norm_g, w_ukv, w_o, ln1_g, ln1_b, w_up, conv_w, conv_b, w_down, ln2_g, ln2_b):
    depth = w_in.shape[0]
    alpha = (2.0 * depth) ** 0.25
    for l in range(depth):
        qat, ka, vat, qmt, km, vmt, kmean = _input_projection(
            x, w_in[l], q_norm_g[l], w_uq[l], kv_norm_g[l], w_ukv[l])
        a_out = _attention(qat, ka, vat)
        b_out = _attention(qmt, km, vmt, kmean)
        x = _mix_and_ffn(x, a_out, b_out, w_o[l], ln1_g[l], ln1_b[l], w_up[l], conv_w[l], conv_b[l],
                         w_down[l], ln2_g[l], ln2_b[l], alpha)
    return x
```

```python
import functools
import math

import jax
import jax.numpy as jnp
from jax import lax
from jax.experimental import pallas as pl
from jax.experimental.pallas import tpu as pltpu

MLA_HEADS = 8
MLA_NOPE = 64
MLA_ROPE = 32
MLA_V = 64
ROPE_THETA = 10000.0
MOBA_HEADS = 8
MOBA_HD = 64
MOBA_BLOCK = 256
MOBA_TOPK = 3
CONV_W = 3
EPS = 1e-5

HEAD_PAD = 128
HEADS_PER_GROUP = HEAD_PAD // MOBA_HD
BLOCKS_PER_TILE = 2
ATTN_TILE = BLOCKS_PER_TILE * MOBA_BLOCK
SLOPE_PIECES = 3
SUM_ROWS = 16
MASKED = -1e30
LOG2E = math.log2(math.e)
PROJ_ROWS = 512
FFN_ROWS = 512
FFN_PARTS = 2
FFN_CHUNK = 256
CARRY_ROWS = 8
VMEM_LIMIT = 48 * 1024 * 1024
BOTH_ATTN_VMEM_LIMIT = 60 * 1024 * 1024

F32 = jnp.float32
BF16 = jnp.bfloat16

_dot = functools.partial(jnp.dot, preferred_element_type=F32)


def _rms_norm(x, g):
    ms = jnp.mean(x * x, axis=-1, keepdims=True)
    return x * lax.rsqrt(ms + EPS) * g


def _layer_norm(x, g, b):
    mu = jnp.mean(x, axis=-1, keepdims=True)
    xc = x - mu
    var = jnp.mean(xc * xc, axis=-1, keepdims=True)
    return xc * lax.rsqrt(var + EPS) * g + b


def _dot_t(w_t, a):
    return lax.dot_general(w_t, a, (((1,), (1,)), ((), ())), preferred_element_type=F32)


def _proj_kernel(x_ref, win_ref, wqm_t_ref, wvm_t_ref, qg_ref, wqa_t_ref, kvg_ref, wk_ref, wv_t_ref,
                 cos_t_ref, sin_t_ref, tk1_ref, tk2_ref,
                 qat_ref, ka_ref, vat_ref, qmt_ref, km_ref, vmt_ref, kmean_ref):
    q_lora = wqa_t_ref.shape[1]
    kv_lora = wk_ref.shape[0]
    moba_w = km_ref.shape[2]
    rows = x_ref.shape[1]
    c0 = q_lora + kv_lora
    c1 = c0 + HEAD_PAD

    xb = x_ref[0].astype(BF16)
    c_q = _dot(xb, win_ref[:, 0:q_lora])
    kv_rope = _dot(xb, win_ref[:, q_lora:c1])
    c_kv = kv_rope[:, 0:kv_lora]
    rope_grp = kv_rope[:, kv_lora:kv_lora + HEAD_PAD]

    cqn = _rms_norm(c_q, qg_ref[...]).astype(BF16)
    ckn = _rms_norm(c_kv, kvg_ref[...]).astype(BF16)
    qa_t = _dot_t(wqa_t_ref[...], cqn)
    kk = _dot(ckn, wk_ref[...])
    vv_t = _dot_t(wv_t_ref[...], ckn)
    qm_t = _dot_t(wqm_t_ref[...], xb) * (MOBA_HD ** -0.5)
    km = _dot(xb, win_ref[:, c1:c1 + moba_w])
    vm_t = _dot_t(wvm_t_ref[...], xb)

    k_pe = rope_grp * tk1_ref[...] + pltpu.roll(rope_grp, HEAD_PAD - MLA_ROPE, 1) * tk2_ref[...]
    half = MLA_ROPE // 2
    cos_t, sin_t = cos_t_ref[...], sin_t_ref[...]
    scale = (MLA_NOPE + MLA_ROPE) ** -0.5 * LOG2E
    for h in range(MLA_HEADS):
        sl = slice(h * HEAD_PAD, (h + 1) * HEAD_PAD)
        base = h * HEAD_PAD
        x1 = qa_t[base + MLA_NOPE:base + MLA_NOPE + half, :]
        x2 = qa_t[base + MLA_NOPE + half:base + MLA_NOPE + MLA_ROPE, :]
        q_h = jnp.concatenate([qa_t[base:base + MLA_NOPE, :] * scale,
                               x1 * cos_t - x2 * sin_t, x2 * cos_t + x1 * sin_t,
                               qa_t[base + MLA_NOPE + MLA_ROPE:base + HEAD_PAD, :]], axis=0)
        qat_ref[0, sl, :] = q_h.astype(BF16)
        ka_ref[0, :, sl] = (kk[:, sl] + k_pe).astype(BF16)

    qmt_ref[0] = qm_t.astype(BF16)
    km_ref[0] = (km * LOG2E).astype(BF16)
    for j in range(rows // MOBA_BLOCK):
        blk = slice(j * MOBA_BLOCK, (j + 1) * MOBA_BLOCK)
        vat_ref[0, j] = vv_t[:, blk].astype(BF16)
        vmt_ref[0, j] = vm_t[:, blk].astype(BF16)
        kmean_ref[0, j] = jnp.mean(km[blk, :], axis=0, keepdims=True)


def _rope_tables(seq):
    half = MLA_ROPE // 2
    inv = ROPE_THETA ** (-jnp.arange(half, dtype=F32) / half)
    ang = jnp.arange(seq, dtype=F32)[:, None] * inv[None, :]
    cos, sin = jnp.cos(ang), jnp.sin(ang)
    z = lambda w: jnp.zeros((seq, w), F32)
    tail = HEAD_PAD - MLA_NOPE - MLA_ROPE
    tk1 = jnp.concatenate([z(MLA_NOPE), cos, cos, z(tail)], axis=1)
    tk2 = jnp.concatenate([z(MLA_NOPE), sin, sin, z(tail)], axis=1)
    scale = (MLA_NOPE + MLA_ROPE) ** -0.5 * LOG2E
    return (cos * scale).T, (sin * scale).T, tk1, tk2


def _rotate_half_cols(w):
    half = w.shape[-1] // 2
    return jnp.concatenate([-w[..., half:], w[..., :half]], axis=-1)


def _prep_proj_weights(w_in, w_uq, w_ukv):
    d_model = w_in.shape[0]
    q_lora = w_uq.shape[0]
    kv_lora = w_ukv.shape[0]
    moba_w = MOBA_HEADS * MOBA_HD
    o = q_lora + kv_lora
    k_rope = w_in[:, o:o + MLA_ROPE]
    rope_grp = jnp.concatenate(
        [jnp.zeros((d_model, MLA_NOPE), w_in.dtype), k_rope, _rotate_half_cols(k_rope)], axis=1)
    o += MLA_ROPE
    wqm, wkm, wvm = (w_in[:, o + j * moba_w:o + (j + 1) * moba_w] for j in range(3))
    win = jnp.concatenate([w_in[:, :q_lora + kv_lora], rope_grp, wkm], axis=1)

    wq = w_uq.reshape(q_lora, MLA_HEADS, MLA_NOPE + MLA_ROPE)
    pad = jnp.zeros((q_lora, MLA_HEADS, HEAD_PAD - MLA_NOPE - MLA_ROPE), w_uq.dtype)
    wqa = jnp.concatenate([wq, pad], axis=-1).reshape(q_lora, MLA_HEADS * HEAD_PAD)

    wkv = w_ukv.reshape(kv_lora, MLA_HEADS, MLA_NOPE + MLA_V)
    wk = jnp.concatenate([wkv[..., :MLA_NOPE],
                          jnp.zeros((kv_lora, MLA_HEADS, HEAD_PAD - MLA_NOPE), w_ukv.dtype)], axis=-1)
    wk = wk.reshape(kv_lora, MLA_HEADS * HEAD_PAD)
    wv = wkv[..., MLA_NOPE:].reshape(kv_lora, MLA_HEADS * MLA_V)
    return (win.astype(BF16), wqm.T.astype(BF16), wvm.T.astype(BF16),
            wqa.T.astype(BF16), wk.astype(BF16), wv.T.astype(BF16))


def _const_spec(shape):
    return pl.BlockSpec(shape, lambda *_: (0,) * len(shape))


def _input_projection(x, w_in, q_norm_g, w_uq, kv_norm_g, w_ukv):
    B, S, D = x.shape
    rows = PROJ_ROWS
    assert S % rows == 0 and rows % MOBA_BLOCK == 0
    nb = S // MOBA_BLOCK
    win, wqm_t, wvm_t, wqa_t, wk, wv_t = _prep_proj_weights(w_in, w_uq, w_ukv)
    cos_t, sin_t, tk1, tk2 = _rope_tables(S)
    mla_w = MLA_HEADS * HEAD_PAD
    v_w = MLA_HEADS * MLA_V
    moba_w = MOBA_HEADS * MOBA_HD
    blocks_per_step = rows // MOBA_BLOCK

    q_table_spec = pl.BlockSpec((cos_t.shape[0], rows), lambda b, i: (0, i))
    k_table_spec = pl.BlockSpec((rows, HEAD_PAD), lambda b, i: (i, 0))
    in_specs = [
        pl.BlockSpec((1, rows, D), lambda b, i: (b, i, 0)),
        _const_spec(win.shape), _const_spec(wqm_t.shape), _const_spec(wvm_t.shape),
        _const_spec((1, q_norm_g.shape[-1])), _const_spec(wqa_t.shape),
        _const_spec((1, kv_norm_g.shape[-1])), _const_spec(wk.shape), _const_spec(wv_t.shape),
        q_table_spec, q_table_spec, k_table_spec, k_table_spec,
    ]
    out_shape = [
        jax.ShapeDtypeStruct((B, mla_w, S), BF16),
        jax.ShapeDtypeStruct((B, S, mla_w), BF16),
        jax.ShapeDtypeStruct((B, nb, v_w, MOBA_BLOCK), BF16),
        jax.ShapeDtypeStruct((B, moba_w, S), BF16),
        jax.ShapeDtypeStruct((B, S, moba_w), BF16),
        jax.ShapeDtypeStruct((B, nb, moba_w, MOBA_BLOCK), BF16),
        jax.ShapeDtypeStruct((B, nb, 1, moba_w), F32),
    ]
    out_specs = [
        pl.BlockSpec((1, mla_w, rows), lambda b, i: (b, 0, i)),
        pl.BlockSpec((1, rows, mla_w), lambda b, i: (b, i, 0)),
        pl.BlockSpec((1, blocks_per_step, v_w, MOBA_BLOCK), lambda b, i: (b, i, 0, 0)),
        pl.BlockSpec((1, moba_w, rows), lambda b, i: (b, 0, i)),
        pl.BlockSpec((1, rows, moba_w), lambda b, i: (b, i, 0)),
        pl.BlockSpec((1, blocks_per_step, moba_w, MOBA_BLOCK), lambda b, i: (b, i, 0, 0)),
        pl.BlockSpec((1, blocks_per_step, 1, moba_w), lambda b, i: (b, i, 0, 0)),
    ]
    return pl.pallas_call(
        _proj_kernel,
        grid=(B, S // rows),
        in_specs=in_specs,
        out_specs=out_specs,
        out_shape=out_shape,
        compiler_params=pltpu.CompilerParams(
            dimension_semantics=("parallel", "parallel"), vmem_limit_bytes=VMEM_LIMIT),
        name="input_projection",
    )(x, win, wqm_t, wvm_t, q_norm_g.reshape(1, -1), wqa_t, kv_norm_g.reshape(1, -1), wk, wv_t,
      cos_t, sin_t, tk1, tk2)


def _moba_bias(kmean, qt, tile, slope):
    nb = kmean.shape[0]
    g = _dot(kmean, qt)
    blk = lax.broadcasted_iota(jnp.int32, g.shape, 0)
    blk_f = blk.astype(F32)
    own = tile * BLOCKS_PER_TILE + lax.broadcasted_iota(jnp.int32, g.shape, 1) // MOBA_BLOCK
    past = blk < own
    g = jnp.where(past, g, -jnp.inf)
    picked = blk < 0
    for _ in range(MOBA_TOPK):
        top = jnp.max(g, axis=0, keepdims=True)
        first = jnp.min(jnp.where(g == top, blk_f, float(nb)), axis=0, keepdims=True)
        hit = blk_f == first
        picked = picked | hit
        g = jnp.where(hit, -jnp.inf, g)
    keep = (picked & past) | (blk == own)
    tile_dist = ((tile - blk // BLOCKS_PER_TILE) * ATTN_TILE).astype(F32)
    return jnp.where(keep, 0.0, MASKED) - slope * tile_dist


def _slope_table(heads):
    slopes = LOG2E * 2.0 ** (-8.0 * jnp.arange(1, heads + 1, dtype=F32) / heads)
    cols, rest = [slopes], slopes
    for _ in range(SLOPE_PIECES):
        piece = rest.astype(BF16).astype(F32)
        cols.append(piece)
        rest = rest - piece
    return jnp.stack(cols, axis=1).reshape(-1)


def _attn_kernel(*refs, moba, heads):
    if moba:
        (slopes_ref, qt_ref, k_ref, vt_ref, kmean_ref, o_ref,
         m_ref, acc_ref, s_ref, smax_ref, qm_ref, bias_ref) = refs
    else:
        qt_ref, k_ref, vt_ref, o_ref, m_ref, acc_ref, s_ref, smax_ref = refs
    tile, blk = ATTN_TILE, MOBA_BLOCK
    assert BLOCKS_PER_TILE == 2
    own = pl.program_id(1)
    own_blk = own * BLOCKS_PER_TILE
    hd = vt_ref.shape[2] // heads
    qd = qt_ref.shape[1] // heads
    lo, hi = slice(0, blk), slice(blk, tile)

    causal = (lax.broadcasted_iota(jnp.int32, (blk, blk), 0)
              <= lax.broadcasted_iota(jnp.int32, (blk, blk), 1))

    if moba:
        r = lax.broadcasted_iota(jnp.int32, (tile, HEAD_PAD), 0)
        lane = lax.broadcasted_iota(jnp.int32, (tile, HEAD_PAD), 1)
        row_hi = (r // blk) * blk
        key_pos = jnp.where(lane < SLOPE_PIECES, row_hi,
                            jnp.where(lane < 2 * SLOPE_PIECES, r - row_hi, 0)).astype(F32).astype(BF16)
        r2 = lax.broadcasted_iota(jnp.int32, (HEAD_PAD, tile), 0)
        for h in range(heads):
            grp = h // HEADS_PER_GROUP
            rows = slice(grp * HEAD_PAD, (grp + 1) * HEAD_PAD)
            first = (h % HEADS_PER_GROUP) * qd
            qt = jnp.where((r2 >= first) & (r2 < first + qd), qt_ref[0, rows, :], jnp.zeros((), BF16))
            slope = slopes_ref[h * (SLOPE_PIECES + 1)]
            slope_rows = jnp.zeros((HEAD_PAD, tile), F32)
            for j in range(SLOPE_PIECES):
                piece = slopes_ref[h * (SLOPE_PIECES + 1) + 1 + j]
                slope_rows = jnp.where((r2 == j) | (r2 == SLOPE_PIECES + j), piece, slope_rows)
            qm_ref[h] = jnp.concatenate([qt, slope_rows.astype(BF16)], axis=0)
            bias = _moba_bias(kmean_ref[0, :, 0, rows].astype(BF16), qt, own, slope)
            for n in range(bias.shape[0]):
                bias_ref[h, n] = bias[n:n + 1, :]

    def keys(h, start, size, pos_rows):
        rows = pl.ds(pl.multiple_of(start, size), size)
        if moba:
            grp = h // HEADS_PER_GROUP
            return jnp.concatenate(
                [k_ref[0, rows, grp * HEAD_PAD:(grp + 1) * HEAD_PAD], key_pos[pos_rows, :]], axis=1)
        return k_ref[0, rows, h * HEAD_PAD:(h + 1) * HEAD_PAD]

    def queries(h):
        return qm_ref[h] if moba else qt_ref[0, h * HEAD_PAD:(h + 1) * HEAD_PAD, :]

    ones = jnp.ones((SUM_ROWS, tile), BF16)

    def values(h, first_block, blocks):
        vs = [vt_ref[0, first_block + j, h * hd:(h + 1) * hd, :] for j in range(blocks)]
        v = vs[0] if blocks == 1 else jnp.concatenate(vs, axis=1)
        return jnp.concatenate([v, ones[:, 0:blocks * blk]], axis=0)

    def issue_diag_scores(h):
        q = queries(h)
        s_ref[h, lo, :] = _dot(keys(h, own * tile, blk, lo), q)
        s_ref[h, hi, hi] = _dot(keys(h, own * tile + blk, blk, hi), q[:, hi])

    def diag_softmax_pv(h):
        s_ll, s_lh, s_hh = s_ref[h, lo, lo], s_ref[h, lo, hi], s_ref[h, hi, hi]
        if moba:
            b_lo, b_hi = bias_ref[h, own_blk], bias_ref[h, own_blk + 1]
            s_ll, s_lh, s_hh = s_ll + b_lo[:, lo], s_lh + b_lo[:, hi], s_hh + b_hi[:, hi]
        s_ll = jnp.where(causal, s_ll, MASKED)
        s_hh = jnp.where(causal, s_hh, MASKED)
        m_l = jnp.max(s_ll, axis=0, keepdims=True)
        m_h = jnp.maximum(jnp.max(s_lh, axis=0, keepdims=True), jnp.max(s_hh, axis=0, keepdims=True))
        p_l = jnp.exp2((s_ll - m_l).astype(BF16))
        p_h = jnp.concatenate([jnp.exp2((s_lh - m_h).astype(BF16)), jnp.exp2((s_hh - m_h).astype(BF16))], axis=0)
        m_ref[h] = jnp.concatenate([m_l, m_h], axis=1)
        acc_ref[h] = jnp.concatenate(
            [_dot(values(h, own_blk, 1), p_l), _dot(values(h, own_blk, BLOCKS_PER_TILE), p_h)], axis=1)

    def issue_scores(h, n):
        s = _dot(keys(h, n * tile, tile, slice(0, tile)), queries(h))
        s_ref[h] = s
        smax_ref[h, 0] = jnp.max(s[lo, :], axis=0, keepdims=True)
        smax_ref[h, 1] = jnp.max(s[hi, :], axis=0, keepdims=True)

    def softmax_pv(h, n):
        s = s_ref[h]
        m_old = m_ref[h]
        if moba:
            b_lo, b_hi = bias_ref[h, n * BLOCKS_PER_TILE], bias_ref[h, n * BLOCKS_PER_TILE + 1]
            m_new = jnp.maximum(m_old, jnp.maximum(smax_ref[h, 0] + b_lo, smax_ref[h, 1] + b_hi))
            p = jnp.concatenate([jnp.exp2((s[lo, :] - (m_new - b_lo)).astype(BF16)),
                                 jnp.exp2((s[hi, :] - (m_new - b_hi)).astype(BF16))], axis=0)
        else:
            m_new = jnp.maximum(m_old, jnp.maximum(smax_ref[h, 0], smax_ref[h, 1]))
            p = jnp.exp2((s - m_new).astype(BF16))
        m_ref[h] = m_new
        acc_ref[h] = jnp.exp2(m_old - m_new) * acc_ref[h] + _dot(values(h, n * BLOCKS_PER_TILE, BLOCKS_PER_TILE), p)

    for h in range(heads):
        issue_diag_scores(h)
    for h in range(heads):
        diag_softmax_pv(h)
        issue_scores(h, 0)

    def past_tile(n, carry):
        for h in range(heads):
            softmax_pv(h, n)
            issue_scores(h, n + 1)
        return carry

    lax.fori_loop(0, own - 1, past_tile, 0)

    @pl.when(own > 0)
    def _():
        for h in range(heads):
            softmax_pv(h, own - 1)

    out_t = jnp.concatenate(
        [acc_ref[h, 0:hd, :] / acc_ref[h, hd:hd + 1, :] for h in range(heads)], axis=0)
    o_ref[0] = out_t.astype(o_ref.dtype)


def _attention(qt, k, vt, kmean=None):
    moba = kmean is not None
    B, q_w, S = qt.shape
    nb, v_w = vt.shape[1], vt.shape[2]
    heads = MOBA_HEADS if moba else MLA_HEADS
    tile = ATTN_TILE
    assert S % tile == 0 and vt.shape[3] == MOBA_BLOCK
    k_w = k.shape[2]

    in_specs = [
        pl.BlockSpec((1, q_w, tile), lambda b, i, *_: (b, 0, i)),
        pl.BlockSpec((1, S, k_w), lambda b, i, *_: (b, 0, 0)),
        pl.BlockSpec((1, nb, v_w, MOBA_BLOCK), lambda b, i, *_: (b, 0, 0, 0)),
    ]
    args = [qt, k, vt]
    scratch = [
        pltpu.VMEM((heads, 1, tile), F32),
        pltpu.VMEM((heads, v_w // heads + SUM_ROWS, tile), F32),
        pltpu.VMEM((heads, tile, tile), F32),
        pltpu.VMEM((heads, BLOCKS_PER_TILE, 1, tile), F32),
    ]
    num_prefetch = 0
    if moba:
        in_specs.append(pl.BlockSpec((1, nb, 1, k_w), lambda b, i, *_: (b, 0, 0, 0)))
        args = [_slope_table(heads)] + args + [kmean]
        scratch += [
            pltpu.VMEM((heads, 2 * HEAD_PAD, tile), BF16),
            pltpu.VMEM((heads, nb, 1, tile), F32),
        ]
        num_prefetch = 1
    grid_spec = pltpu.PrefetchScalarGridSpec(
        num_scalar_prefetch=num_prefetch,
        grid=(B, S // tile),
        in_specs=in_specs,
        out_specs=pl.BlockSpec((1, v_w, tile), lambda b, i, *_: (b, 0, i)),
        scratch_shapes=scratch,
    )
    return pl.pallas_call(
        functools.partial(_attn_kernel, moba=moba, heads=heads),
        grid_spec=grid_spec,
        out_shape=jax.ShapeDtypeStruct((B, v_w, S), BF16),
        compiler_params=pltpu.CompilerParams(
            dimension_semantics=("parallel", "arbitrary"), vmem_limit_bytes=VMEM_LIMIT),
        name="moba_attention" if moba else "mla_attention",
    )(*args)


def _both_attn_kernel(slopes_ref, qa_ref, ka_ref, va_ref, qm_ref_in, km_ref, vm_ref, kmean_ref,
                      oa_ref, om_ref, m_a, acc_a, m_m, acc_m, s_ref, smax_ref, qmask_ref, bias_ref):
    _attn_kernel(qa_ref, ka_ref, va_ref, oa_ref, m_a, acc_a, s_ref, smax_ref, moba=False, heads=MLA_HEADS)
    _attn_kernel(slopes_ref, qm_ref_in, km_ref, vm_ref, kmean_ref, om_ref, m_m, acc_m, s_ref, smax_ref,
                 qmask_ref, bias_ref, moba=True, heads=MOBA_HEADS)


def _both_attentions(qat, ka, vat, qmt, km, vmt, kmean):
    assert MLA_HEADS == MOBA_HEADS
    B, _, S = qat.shape
    nb = vat.shape[1]
    tile, heads = ATTN_TILE, MLA_HEADS
    assert S % tile == 0 and vat.shape[3] == MOBA_BLOCK and vmt.shape[3] == MOBA_BLOCK

    q_spec = lambda a: pl.BlockSpec((1, a.shape[1], tile), lambda b, i, *_: (b, 0, i))
    k_spec = lambda a: pl.BlockSpec((1, S, a.shape[2]), lambda b, i, *_: (b, 0, 0))
    v_spec = lambda a: pl.BlockSpec((1, nb, a.shape[2], MOBA_BLOCK), lambda b, i, *_: (b, 0, 0, 0))
    o_spec = lambda a: pl.BlockSpec((1, a.shape[2], tile), lambda b, i, *_: (b, 0, i))
    state = lambda a: [pltpu.VMEM((heads, 1, tile), F32),
                       pltpu.VMEM((heads, a.shape[2] // heads + SUM_ROWS, tile), F32)]
    grid_spec = pltpu.PrefetchScalarGridSpec(
        num_scalar_prefetch=1,
        grid=(B, S // tile),
        in_specs=[q_spec(qat), k_spec(ka), v_spec(vat), q_spec(qmt), k_spec(km),
                  pl.BlockSpec((1, nb, vmt.shape[2], MOBA_BLOCK), lambda b, i, *_: (b, 0, 0, 0),
                               pipeline_mode=pl.Buffered(1)),
                  pl.BlockSpec((1, nb, 1, km.shape[2]), lambda b, i, *_: (b, 0, 0, 0))],
        out_specs=[o_spec(vat), o_spec(vmt)],
        scratch_shapes=state(vat) + state(vmt) + [
            pltpu.VMEM((heads, tile, tile), F32),
            pltpu.VMEM((heads, BLOCKS_PER_TILE, 1, tile), F32),
            pltpu.VMEM((heads, 2 * HEAD_PAD, tile), BF16),
            pltpu.VMEM((heads, nb, 1, tile), F32),
        ],
    )
    return pl.pallas_call(
        _both_attn_kernel,
        grid_spec=grid_spec,
        out_shape=[jax.ShapeDtypeStruct((B, vat.shape[2], S), BF16),
                   jax.ShapeDtypeStruct((B, vmt.shape[2], S), BF16)],
        compiler_params=pltpu.CompilerParams(
            dimension_semantics=("parallel", "arbitrary"), vmem_limit_bytes=BOTH_ATTN_VMEM_LIMIT),
        name="mla_moba_attention",
    )(_slope_table(heads), qat, ka, vat, qmt, km, vmt, kmean)


def _ffn_kernel(x_ref, a_ref, b_ref, woa_ref, wob_ref, g1_ref, b1_ref,
                wup_ref, cw_ref, cb_ref, wdn_ref, g2_ref, b2_ref,
                o_ref, act_ref, carry_ref, *, alpha):
    rows = x_ref.shape[1]
    d_ff = wdn_ref.shape[0]
    chunk = FFN_CHUNK

    @pl.when(pl.program_id(1) == 0)
    def _():
        carry_ref[...] = jnp.zeros(carry_ref.shape, F32)

    part_rows = rows // FFN_PARTS
    parts = [slice(p * part_rows, (p + 1) * part_rows) for p in range(FFN_PARTS)]

    dot_at = lambda a_t, w: lax.dot_general(a_t, w, (((0,), (0,)), ((), ())), preferred_element_type=F32)
    mixes = [dot_at(a_ref[0, :, r], woa_ref[...]) + dot_at(b_ref[0, :, r], wob_ref[...]) for r in parts]
    x1 = [_layer_norm(alpha * x_ref[0, r, :] + mix, g1_ref[...], b1_ref[...]) for r, mix in zip(parts, mixes)]
    x1b = [v.astype(BF16) for v in x1]

    def conv_branch(p, col, prev):
        cols = slice(col, col + chunk)
        h = _dot(x1b[p], wup_ref[:, cols])
        cw = cw_ref[:, cols]
        y = h * cw[CONV_W - 1:CONV_W, :] + cb_ref[:, cols]
        top = lax.broadcasted_iota(jnp.int32, (CARRY_ROWS, chunk), 0)
        for tap in range(CONV_W - 1):
            back = CONV_W - 1 - tap
            moved = pltpu.roll(h, back, 0)
            head = jnp.where(top < back, pltpu.roll(prev, back, 0), moved[0:CARRY_ROWS, :])
            shifted = jnp.concatenate([head, moved[CARRY_ROWS:, :]], axis=0)
            y = y + shifted * cw[tap:tap + 1, :]
        return y, h[part_rows - CARRY_ROWS:, :]

    for c in range(d_ff // chunk):
        g_col, u_col = c * chunk, d_ff + c * chunk
        g_prev = carry_ref[:, g_col:g_col + chunk]
        u_prev = carry_ref[:, u_col:u_col + chunk]
        for p, r in enumerate(parts):
            gate, g_prev = conv_branch(p, g_col, g_prev)
            up, u_prev = conv_branch(p, u_col, u_prev)
            act = gate * (1.0 / (1.0 + jnp.exp(-gate))) * up
            act_ref[r, c * chunk:(c + 1) * chunk] = act.astype(BF16)
        carry_ref[:, g_col:g_col + chunk] = g_prev
        carry_ref[:, u_col:u_col + chunk] = u_prev

    ys = [_dot(act_ref[r, :], wdn_ref[...]) for r in parts]
    for r, v, y in zip(parts, x1, ys):
        o_ref[0, r, :] = _layer_norm(alpha * v + y, g2_ref[...], b2_ref[...])


def _mix_and_ffn(x, a_out, b_out, w_o, ln1_g, ln1_b, w_up, conv_w, conv_b, w_down, ln2_g, ln2_b, alpha):
    B, S, D = x.shape
    rows = FFN_ROWS
    d_ff = w_down.shape[0]
    assert S % rows == 0 and d_ff % FFN_CHUNK == 0 and conv_w.shape[0] == CONV_W
    a_w = a_out.shape[1]
    row_spec = lambda w: pl.BlockSpec((1, rows, w), lambda b, i: (b, i, 0))
    col_spec = lambda w: pl.BlockSpec((1, w, rows), lambda b, i: (b, 0, i))
    vec = lambda v: v.reshape(1, -1).astype(F32)
    in_specs = [
        row_spec(D), col_spec(a_w), col_spec(b_out.shape[1]),
        _const_spec((a_w, D)), _const_spec((w_o.shape[0] - a_w, D)),
        _const_spec((1, D)), _const_spec((1, D)),
        _const_spec(w_up.shape), _const_spec(conv_w.shape), _const_spec((1, 2 * d_ff)),
        _const_spec(w_down.shape), _const_spec((1, D)), _const_spec((1, D)),
    ]
    return pl.pallas_call(
        functools.partial(_ffn_kernel, alpha=alpha),
        grid=(B, S // rows),
        in_specs=in_specs,
        out_specs=row_spec(D),
        out_shape=jax.ShapeDtypeStruct((B, S, D), x.dtype),
        scratch_shapes=[
            pltpu.VMEM((rows, d_ff), BF16),
            pltpu.VMEM((CARRY_ROWS, 2 * d_ff), F32),
        ],
        compiler_params=pltpu.CompilerParams(
            dimension_semantics=("parallel", "arbitrary"), vmem_limit_bytes=VMEM_LIMIT),
        name="mix_ffn",
    )(x, a_out, b_out, w_o[:a_w].astype(BF16), w_o[a_w:].astype(BF16), vec(ln1_g), vec(ln1_b),
      w_up.astype(BF16), conv_w.astype(F32), vec(conv_b), w_down.astype(BF16), vec(ln2_g), vec(ln2_b))


def kernel(x, w_in, q_norm_g, w_uq, kv_norm_g, w_ukv, w_o, ln1_g, ln1_b, w_up, conv_w, conv_b, w_down, ln2_g, ln2_b):
    depth = w_in.shape[0]
    alpha = (2.0 * depth) ** 0.25
    for l in range(depth):
        qat, ka, vat, qmt, km, vmt, kmean = _input_projection(
            x, w_in[l], q_norm_g[l], w_uq[l], kv_norm_g[l], w_ukv[l])
        a_out, b_out = _both_attentions(qat, ka, vat, qmt, km, vmt, kmean)
        x = _mix_and_ffn(x, a_out, b_out, w_o[l], ln1_g[l], ln1_b[l], w_up[l], conv_w[l], conv_b[l],
                         w_down[l], ln2_g[l], ln2_b[l], alpha)
    return x
```
